```python
import jax, jax.numpy as jnp
from jax import lax
import numpy as np

D_MODEL = 2048
BATCH = 4
SEQ = 2048
DEPTH = 4
DEC_BATCH = 8
DEC_SEQ = 4
PAST_LEN = 16384
PAGE_SIZE = 128

HEAD_DIM = 128
H_A = D_MODEL // (4 * HEAD_DIM)
H_B = D_MODEL // (2 * HEAD_DIM)
H_C = D_MODEL // (4 * HEAD_DIM)
D_A = H_A * HEAD_DIM
D_B = H_B * HEAD_DIM
D_C = H_C * HEAD_DIM
D_MIX = D_A + D_B + D_C
D_IN = 2 * D_A + 3 * D_B + 3 * D_C
CHUNK = 128
Q_BLOCK = 128
CONV_W = 3
D_FF = 5632
N_EXPERTS = 8
TOP_K = 2
D_EXPERT = 7168
PLE_DIM = 256
SB_BIAS_INIT = -8.0
LN_EPS = 1e-5
RMS_EPS = 1e-6
ALPHA = (2 * DEPTH) ** 0.25
BETA = (8 * DEPTH) ** -0.25
SPLIT_POINTS = [D_A, 2 * D_A, 2 * D_A + D_B, 2 * D_A + 2 * D_B, 2 * D_A + 3 * D_B,
                2 * D_A + 3 * D_B + D_C, 2 * D_A + 3 * D_B + 2 * D_C]

kernel_name = "hybrid_gmlp_stickbreak_shortconv_step"


def layer_norm(x, g, b):
    xf = x.astype(jnp.float32)
    mu = jnp.mean(xf, axis=-1, keepdims=True)
    var = jnp.mean(jnp.square(xf - mu), axis=-1, keepdims=True)
    return ((xf - mu) * lax.rsqrt(var + LN_EPS) * g.astype(jnp.float32) + b.astype(jnp.float32)).astype(x.dtype)


def rms_norm(x, g):
    xf = x.astype(jnp.float32)
    ms = jnp.mean(jnp.square(xf), axis=-1, keepdims=True)
    return (xf * lax.rsqrt(ms + RMS_EPS) * g.astype(jnp.float32)).astype(x.dtype)


def chunk_gmlp(u, v, w_s, b_s):
    bsz, t, _ = v.shape
    L = min(CHUNK, t)
    n_c = -(-t // L)
    pad = n_c * L - t
    vp = jnp.pad(v, ((0, 0), (0, pad), (0, 0))).reshape(bsz, n_c, L, H_A, HEAD_DIM)
    w = jnp.tril(w_s[:, :L, :L])
    mixed = jnp.einsum("hij,bcjhd->bcihd", w, vp) + b_s[:, :L].T[None, None, :, :, None]
    mixed = mixed.reshape(bsz, n_c * L, D_A)[:, :t]
    return u * mixed


def stick_breaking_attention(q, k, v, bias, q_offset):
    bsz, tq, h, dh = q.shape
    tk = k.shape[1]
    blk = min(Q_BLOCK, tq)
    n_blk = -(-tq // blk)
    pad = n_blk * blk - tq
    qb = jnp.pad(q, ((0, 0), (0, pad), (0, 0), (0, 0))).astype(jnp.float32)
    qb = qb.reshape(bsz, n_blk, blk, h, dh).transpose(1, 0, 2, 3, 4)
    kf = k.astype(jnp.float32)
    vf = v.astype(jnp.float32)
    bf = bias.astype(jnp.float32)[None, :, None, None]
    key_pos = jnp.arange(tk)
    scale = dh ** -0.5

    def one_block(args):
        q_blk, start = args
        z = jnp.einsum("bqhd,bkhd->bhqk", q_blk, kf) * scale + bf
        q_pos = q_offset + start + jnp.arange(blk)
        mask = key_pos[None, :] < q_pos[:, None]
        log_rest = jnp.where(mask, jax.nn.log_sigmoid(-z), 0.0)
        suffix = lax.cumsum(log_rest, axis=3, reverse=True) - log_rest
        a = jnp.where(mask, jnp.exp(jax.nn.log_sigmoid(z) + suffix), 0.0)
        return jnp.einsum("bhqk,bkhd->bqhd", a, vf)

    starts = jnp.arange(n_blk, dtype=jnp.int32) * blk
    out = lax.map(one_block, (qb, starts))
    out = out.transpose(1, 0, 2, 3, 4).reshape(bsz, n_blk * blk, h, dh)[:, :tq]
    return out.astype(q.dtype)


def short_conv(xc, prev, w):
    t = xc.shape[1]
    xp = jnp.concatenate([prev, xc], axis=1)
    y = w[CONV_W - 1] * xp[:, CONV_W - 1:CONV_W - 1 + t]
    for j in range(CONV_W - 1):
        y = y + w[j] * xp[:, j:j + t]
    return y, xp[:, t:]


def token_mix(x, k_past, v_past, conv_prev, q_offset,
              w_in_i, w_s_i, b_s_i, lnv_g_i, lnv_b_i, conv_w_i, sb_bias_i, onorm_g_i, w_out_i):
    bsz, t, _ = x.shape
    proj = x @ w_in_i
    u_a, v_a, q, k, v, gate_b, gate_c, xc = jnp.split(proj, SPLIT_POINTS, axis=-1)
    u_a = jax.nn.gelu(u_a, approximate=False)
    v_a = layer_norm(jax.nn.gelu(v_a, approximate=False), lnv_g_i, lnv_b_i)
    y_a = chunk_gmlp(u_a, v_a, w_s_i, b_s_i)
    q = q.reshape(bsz, t, H_B, HEAD_DIM)
    k = k.reshape(bsz, t, H_B, HEAD_DIM)
    v = v.reshape(bsz, t, H_B, HEAD_DIM)
    k_all = jnp.concatenate([k_past, k], axis=1)
    v_all = jnp.concatenate([v_past, v], axis=1)
    y_b = stick_breaking_attention(q, k_all, v_all, sb_bias_i, q_offset).reshape(bsz, t, D_B)
    conv_out, conv_new = short_conv(gate_c * xc, conv_prev, conv_w_i)
    y_c = gate_b * conv_out
    y = jnp.concatenate([rms_norm(y_a, onorm_g_i[:D_A]),
                         rms_norm(y_b, onorm_g_i[D_A:D_A + D_B]),
                         rms_norm(y_c, onorm_g_i[D_A + D_B:])], axis=-1)
    return y @ w_out_i, k, v, v_a, conv_new


def swiglu(x, wg, wu, wd):
    return (jax.nn.silu(x @ wg) * (x @ wu)) @ wd


def moe_swiglu(x, w_r, wg, wu, wd):
    logits = (x @ w_r).astype(jnp.float32)
    top_v, top_i = lax.top_k(logits, TOP_K)
    gates = jax.nn.softmax(top_v, axis=-1)
    combine = jnp.sum(jax.nn.one_hot(top_i, N_EXPERTS, dtype=jnp.float32) * gates[..., None], axis=-2)
    out = jnp.zeros_like(x)
    for e in range(N_EXPERTS):
        out = out + combine[..., e:e + 1].astype(x.dtype) * swiglu(x, wg[e], wu[e], wd[e])
    return out


def layer_tail(x, h, p_i, i, ln1_g, ln1_b, ln2_g, ln2_b, w_ffn_gate, w_ffn_up, w_ffn_down,
               w_router, w_exp_gate, w_exp_up, w_exp_down, w_ple_gate, w_ple_proj):
    x = layer_norm(ALPHA * x + h, ln1_g[i], ln1_b[i])
    j = i // 2
    if i % 2 == 0:
        f = swiglu(x, w_ffn_gate[j], w_ffn_up[j], w_ffn_down[j])
    else:
        f = moe_swiglu(x, w_router[j], w_exp_gate[j], w_exp_up[j], w_exp_down[j])
    x = layer_norm(ALPHA * x + f, ln2_g[i], ln2_b[i])
    return x + jax.nn.sigmoid(x @ w_ple_gate[i]) * (p_i @ w_ple_proj[i])


def setup_inputs(seed: int = 0) -> dict:
    key = jax.random.key(seed)
    ks = jax.random.split(key, 40)
    n_pages = PAST_LEN // PAGE_SIZE
    n_used = DEC_BATCH * n_pages
    n_pool = -(-5 * n_used // 4)
    n_dense = (DEPTH + 1) // 2
    n_moe = DEPTH // 2
    f32 = jnp.float32
    nrm = lambda k, shape, s: jax.random.normal(k, shape, f32) * s
    page_table = jax.random.permutation(ks[0], n_pool)[:n_used].reshape(DEC_BATCH, n_pages).astype(jnp.int32)
    return {
        "x_prompt": nrm(ks[1], (BATCH, SEQ, D_MODEL), 1.0),
        "x_sample": nrm(ks[2], (DEC_BATCH, DEC_SEQ, D_MODEL), 1.0),
        "cache_k": nrm(ks[3], (n_pool, DEPTH, PAGE_SIZE, H_B, HEAD_DIM), 1.0),
        "cache_v": nrm(ks[4], (n_pool, DEPTH, PAGE_SIZE, H_B, HEAD_DIM), 1.0),
        "state_conv": nrm(ks[5], (DEC_BATCH, DEPTH, CONV_W - 1, D_C), 1.0),
        "page_table": page_table,
        "p_prompt": nrm(ks[6], (DEPTH, BATCH, SEQ, PLE_DIM), 1.0),
        "p_sample": nrm(ks[7], (DEPTH, DEC_BATCH, DEC_SEQ, PLE_DIM), 1.0),
        "w_in": nrm(ks[8], (DEPTH, D_MODEL, D_IN), D_MODEL ** -0.5),
        "w_spatial": nrm(ks[9], (DEPTH, H_A, CHUNK, CHUNK), 0.5 * CHUNK ** -0.5),
        "b_spatial": 1.0 + nrm(ks[10], (DEPTH, H_A, CHUNK), 0.01),
        "lnv_g": 1.0 + nrm(ks[11], (DEPTH, D_A), 0.01),
        "lnv_b": nrm(ks[12], (DEPTH, D_A), 0.01),
        "conv_w": nrm(ks[13], (DEPTH, CONV_W, D_C), CONV_W ** -0.5),
        "sb_bias": SB_BIAS_INIT + nrm(ks[29], (DEPTH, H_B), 0.1),
        "out_norm_g": 1.0 + nrm(ks[14], (DEPTH, D_MIX), 0.01),
        "w_out": nrm(ks[15], (DEPTH, D_MIX, D_MODEL), BETA * D_MIX ** -0.5),
        "ln1_g": 1.0 + nrm(ks[16], (DEPTH, D_MODEL), 0.01),
        "ln1_b": nrm(ks[17], (DEPTH, D_MODEL), 0.01),
        "ln2_g": 1.0 + nrm(ks[18], (DEPTH, D_MODEL), 0.01),
        "ln2_b": nrm(ks[19], (DEPTH, D_MODEL), 0.01),
        "w_ffn_gate": nrm(ks[20], (n_dense, D_MODEL, D_FF), D_MODEL ** -0.5),
        "w_ffn_up": nrm(ks[21], (n_dense, D_MODEL, D_FF), D_MODEL ** -0.5),
        "w_ffn_down": nrm(ks[22], (n_dense, D_FF, D_MODEL), BETA * D_FF ** -0.5),
        "w_router": nrm(ks[23], (n_moe, D_MODEL, N_EXPERTS), D_MODEL ** -0.5),
        "w_exp_gate": nrm(ks[24], (n_moe, N_EXPERTS, D_MODEL, D_EXPERT), D_MODEL ** -0.5),
        "w_exp_up": nrm(ks[25], (n_moe, N_EXPERTS, D_MODEL, D_EXPERT), D_MODEL ** -0.5),
        "w_exp_down": nrm(ks[26], (n_moe, N_EXPERTS, D_EXPERT, D_MODEL), BETA * D_EXPERT ** -0.5),
        "w_ple_gate": nrm(ks[27], (DEPTH, D_MODEL, D_MODEL), D_MODEL ** -0.5),
        "w_ple_proj": nrm(ks[28], (DEPTH, PLE_DIM, D_MODEL), 0.5 * PLE_DIM ** -0.5),
    }


def reference(x_prompt, x_sample, cache_k, cache_v, state_conv, page_table, p_prompt, p_sample,
              w_in, w_spatial, b_spatial, lnv_g, lnv_b, conv_w, sb_bias, out_norm_g, w_out,
              ln1_g, ln1_b, ln2_g, ln2_b, w_ffn_gate, w_ffn_up, w_ffn_down,
              w_router, w_exp_gate, w_exp_up, w_exp_down, w_ple_gate, w_ple_proj):
    n_pages = PAST_LEN // PAGE_SIZE
    tail_w = (ln1_g, ln1_b, ln2_g, ln2_b, w_ffn_gate, w_ffn_up, w_ffn_down,
              w_router, w_exp_gate, w_exp_up, w_exp_down, w_ple_gate, w_ple_proj)
    empty_kv = jnp.zeros((BATCH, 0, H_B, HEAD_DIM), x_prompt.dtype)
    conv_zero = jnp.zeros((BATCH, CONV_W - 1, D_C), x_prompt.dtype)
    xp, xs = x_prompt, x_sample
    kp_rows, vp_rows, convp_rows = [], [], []
    ks_rows, vs_rows, convs_rows, va_rows = [], [], [], []
    for i in range(DEPTH):
        mix_w = (w_in[i], w_spatial[i], b_spatial[i], lnv_g[i], lnv_b[i], conv_w[i], sb_bias[i],
                 out_norm_g[i], w_out[i])
        h_p, k_p, v_p, _, conv_p = token_mix(xp, empty_kv, empty_kv, conv_zero, 0, *mix_w)
        xp = layer_tail(xp, h_p, p_prompt[i], i, *tail_w)
        kp_rows.append(k_p)
        vp_rows.append(v_p)
        convp_rows.append(conv_p)
        k_past = cache_k[page_table, i].reshape(DEC_BATCH, n_pages * PAGE_SIZE, H_B, HEAD_DIM)
        v_past = cache_v[page_table, i].reshape(DEC_BATCH, n_pages * PAGE_SIZE, H_B, HEAD_DIM)
        h_s, k_s, v_s, va_s, conv_s = token_mix(xs, k_past, v_past, state_conv[:, i], PAST_LEN, *mix_w)
        xs = layer_tail(xs, h_s, p_sample[i], i, *tail_w)
        ks_rows.append(k_s)
        vs_rows.append(v_s)
        convs_rows.append(conv_s)
        va_rows.append(va_s)
    new_k_prompt = jnp.stack(kp_rows, axis=1)
    new_v_prompt = jnp.stack(vp_rows, axis=1)
    new_conv_prompt = jnp.stack(convp_rows, axis=1)
    new_k_sample = jnp.stack(ks_rows, axis=1)
    new_v_sample = jnp.stack(vs_rows, axis=1)
    new_conv_sample = jnp.stack(convs_rows, axis=1)
    chunk_v_sample = jnp.stack(va_rows, axis=1)
    return (xp, xs, new_k_prompt, new_v_prompt, new_conv_prompt,
            new_k_sample, new_v_sample, new_conv_sample, chunk_v_sample)
```

```python
import functools

import jax
import jax.numpy as jnp
from jax import lax
from jax.experimental import pallas as pl
from jax.experimental.pallas import tpu as pltpu

F32 = jnp.float32
BF16 = jnp.bfloat16
I32 = jnp.int32

HEAD_DIM = 128
CHUNK = 128
TOP_K = 2
LN_EPS = 1e-5
RMS_EPS = 1e-6
SB_BLOCK = 128
SB_PROMPT_TILE = 512
SB_SCALE = HEAD_DIM ** -0.5
LANE = 128
MIB = 1024 * 1024
MOE_TILE = 256
SAMPLE_PAGES_PER_STEP = 8


def _cparams(semantics, vmem_mib):
    return pltpu.CompilerParams(dimension_semantics=semantics, vmem_limit_bytes=vmem_mib * MIB)


def _pick_tile(n, pref, align=LANE):
    if n <= pref:
        return n
    best = None
    for t in range(align, pref + 1, align):
        if n % t == 0:
            best = t
    assert best is not None, (n, pref)
    return best


def _row_tile_dispatch(i, n_tiles, tm, m_total, fn):
    rem = m_total - (n_tiles - 1) * tm
    if rem == tm:
        fn(tm)
    else:
        pl.when(i < n_tiles - 1)(lambda: fn(tm))
        pl.when(i == n_tiles - 1)(lambda: fn(rem))


def _layer_norm(xf, g, b):
    mu = jnp.mean(xf, axis=-1, keepdims=True)
    xc = xf - mu
    var = jnp.mean(xc * xc, axis=-1, keepdims=True)
    return xc * lax.rsqrt(var + LN_EPS) * g + b


def _rms_norm(xf, g):
    ms = jnp.mean(xf * xf, axis=-1, keepdims=True)
    return xf * lax.rsqrt(ms + RMS_EPS) * g


def _gelu(x):
    return x * (lax.erf(x / (2.0 ** 0.5)) + 1.0) / 2.0


def _mm_kernel(x_ref, w_ref, o_ref, wb_ref, *, tm, m_total, n_tiles):
    i = pl.program_id(1)

    @pl.when(i == 0)
    def _():
        wb_ref[...] = w_ref[...].astype(BF16)

    def fn(rows):
        acc = jnp.dot(x_ref[0:rows, :], wb_ref[...], preferred_element_type=F32)
        o_ref[0:rows, :] = acc.astype(o_ref.dtype)

    _row_tile_dispatch(i, n_tiles, tm, m_total, fn)


def _mm(x, w_stack, layer, out_dtype, tm, tn, name):
    m, k = x.shape
    n = w_stack.shape[-1]
    tm = min(tm, m)
    tn = _pick_tile(n, tn)
    n_tiles = pl.cdiv(m, tm)
    vmem = (2 * k * tn * 4 + k * tn * 2 + 2 * tm * k * 2 + 3 * tm * tn * 4) // MIB + 6
    return pl.pallas_call(
        functools.partial(_mm_kernel, tm=tm, m_total=m, n_tiles=n_tiles),
        grid=(n // tn, n_tiles),
        in_specs=[pl.BlockSpec((tm, k), lambda j, i: (i, 0)),
                  pl.BlockSpec((None, k, tn), lambda j, i: (layer, 0, j))],
        out_specs=pl.BlockSpec((tm, tn), lambda j, i: (i, j)),
        out_shape=jax.ShapeDtypeStruct((m, n), out_dtype),
        scratch_shapes=[pltpu.VMEM((k, tn), BF16)],
        compiler_params=_cparams(("arbitrary", "arbitrary"), vmem),
        name=name,
    )(x, w_stack)


def _swiglu_up_kernel(x_ref, wg_ref, wu_ref, o_ref, wgb_ref, wub_ref, *, tm, m_total, n_tiles):
    i = pl.program_id(1)

    @pl.when(i == 0)
    def _():
        wgb_ref[...] = wg_ref[...].astype(BF16)
        wub_ref[...] = wu_ref[...].astype(BF16)

    def fn(rows):
        x = x_ref[0:rows, :]
        g = jnp.dot(x, wgb_ref[...], preferred_element_type=F32)
        u = jnp.dot(x, wub_ref[...], preferred_element_type=F32)
        o_ref[0:rows, :] = (g * jax.nn.sigmoid(g) * u).astype(o_ref.dtype)

    _row_tile_dispatch(i, n_tiles, tm, m_total, fn)


def _swiglu_up(x, wg_stack, wu_stack, layer, tm, tn):
    m, k = x.shape
    n = wg_stack.shape[-1]
    tm = min(tm, m)
    tn = _pick_tile(n, tn)
    n_tiles = pl.cdiv(m, tm)
    vmem = (4 * k * tn * 4 + 2 * k * tn * 2 + 2 * tm * k * 2 + 5 * tm * tn * 4) // MIB + 6
    w_spec = pl.BlockSpec((None, k, tn), lambda j, i: (layer, 0, j))
    return pl.pallas_call(
        functools.partial(_swiglu_up_kernel, tm=tm, m_total=m, n_tiles=n_tiles),
        grid=(n // tn, n_tiles),
        in_specs=[pl.BlockSpec((tm, k), lambda j, i: (i, 0)), w_spec, w_spec],
        out_specs=pl.BlockSpec((tm, tn), lambda j, i: (i, j)),
        out_shape=jax.ShapeDtypeStruct((m, n), BF16),
        scratch_shapes=[pltpu.VMEM((k, tn), BF16), pltpu.VMEM((k, tn), BF16)],
        compiler_params=_cparams(("arbitrary", "arbitrary"), vmem),
        name="ffn_up",
    )(x, wg_stack, wu_stack)


def _ple_kernel(xb_ref, xf_ref, p_ref, wg_ref, wp_ref, of_ref, ob_ref, wgb_ref, wpb_ref,
                *, tm, m_total, n_tiles):
    i = pl.program_id(1)

    @pl.when(i == 0)
    def _():
        wgb_ref[...] = wg_ref[...].astype(BF16)
        wpb_ref[...] = wp_ref[...].astype(BF16)

    def fn(rows):
        gate = jnp.dot(xb_ref[0:rows, :], wgb_ref[...], preferred_element_type=F32)
        proj = jnp.dot(p_ref[0:rows, :].astype(BF16), wpb_ref[...], preferred_element_type=F32)
        y = xf_ref[0:rows, :] + jax.nn.sigmoid(gate) * proj
        of_ref[0:rows, :] = y
        ob_ref[0:rows, :] = y.astype(BF16)

    _row_tile_dispatch(i, n_tiles, tm, m_total, fn)


def _ple(xb, xf, p, wg_stack, wp_stack, layer, tm, tn):
    m, d = xf.shape
    pd = p.shape[1]
    tm = min(tm, m)
    tn = _pick_tile(d, tn)
    n_tiles = pl.cdiv(m, tm)
    vmem = (2 * d * tn * 4 + d * tn * 2 + 3 * pd * tn * 4 + 2 * tm * d * 2 + 2 * tm * pd * 4
            + 9 * tm * tn * 4) // MIB + 6
    return pl.pallas_call(
        functools.partial(_ple_kernel, tm=tm, m_total=m, n_tiles=n_tiles),
        grid=(d // tn, n_tiles),
        in_specs=[pl.BlockSpec((tm, d), lambda j, i: (i, 0)),
                  pl.BlockSpec((tm, tn), lambda j, i: (i, j)),
                  pl.BlockSpec((tm, pd), lambda j, i: (i, 0)),
                  pl.BlockSpec((None, d, tn), lambda j, i: (layer, 0, j)),
                  pl.BlockSpec((None, pd, tn), lambda j, i: (layer, 0, j))],
        out_specs=[pl.BlockSpec((tm, tn), lambda j, i: (i, j)),
                   pl.BlockSpec((tm, tn), lambda j, i: (i, j))],
        out_shape=[jax.ShapeDtypeStruct((m, d), F32), jax.ShapeDtypeStruct((m, d), BF16)],
        scratch_shapes=[pltpu.VMEM((d, tn), BF16), pltpu.VMEM((pd, tn), BF16)],
        compiler_params=_cparams(("arbitrary", "arbitrary"), vmem),
        name="ple",
    )(xb, xf, p, wg_stack, wp_stack)


def _top2(logits):
    n_e = logits.shape[-1]
    lane = lax.broadcasted_iota(I32, logits.shape, 1)
    m1 = jnp.max(logits, axis=-1, keepdims=True)
    i1 = jnp.min(jnp.where(logits == m1, lane, n_e), axis=-1, keepdims=True)
    rest = jnp.where(lane == i1, -jnp.inf, logits)
    m2 = jnp.max(rest, axis=-1, keepdims=True)
    i2 = jnp.min(jnp.where(rest == m2, lane, n_e), axis=-1, keepdims=True)
    e2 = jnp.exp(m2 - m1)
    g1 = 1.0 / (1.0 + e2)
    g2 = e2 / (1.0 + e2)
    slot = lax.broadcasted_iota(I32, (logits.shape[0], TOP_K), 1)
    return jnp.where(slot == 0, i1, i2), jnp.where(slot == 0, g1, g2)


def _ln_res_kernel(x_ref, h_ref, g_ref, b_ref, of_ref, ob_ref, *, alpha):
    y = _layer_norm(alpha * x_ref[...] + h_ref[...], g_ref[...], b_ref[...])
    of_ref[...] = y
    ob_ref[...] = y.astype(BF16)


def _ln_res_router_kernel(x_ref, h_ref, g_ref, b_ref, wr_ref, of_ref, ob_ref, idx_ref, gate_ref,
                          *, alpha):
    y = _layer_norm(alpha * x_ref[...] + h_ref[...], g_ref[...], b_ref[...])
    of_ref[...] = y
    ob_ref[...] = y.astype(BF16)
    logits = jnp.dot(y.astype(BF16), wr_ref[...].astype(BF16), preferred_element_type=F32)
    idx, gates = _top2(logits)
    idx_ref[...] = idx
    gate_ref[...] = gates


def _ln_res_combine_kernel(x_ref, y1_ref, y2_ref, gate_ref, g_ref, b_ref, of_ref, ob_ref, *, alpha):
    gates = gate_ref[...]
    f = gates[:, 0:1] * y1_ref[...] + gates[:, 1:2] * y2_ref[...]
    y = _layer_norm(alpha * x_ref[...] + f, g_ref[...], b_ref[...])
    of_ref[...] = y
    ob_ref[...] = y.astype(BF16)


def _ln_res(x, h, g, b, alpha, tm, w_router=None):
    m, d = x.shape
    tm = min(tm, m)
    row = pl.BlockSpec((tm, d), lambda i: (i, 0))
    vec = pl.BlockSpec((1, d), lambda i: (0, 0))
    out_specs = [row, row]
    out_shape = [jax.ShapeDtypeStruct((m, d), F32), jax.ShapeDtypeStruct((m, d), BF16)]
    in_specs = [row, row, vec, vec]
    args = [x, h, g.reshape(1, d), b.reshape(1, d)]
    if w_router is None:
        body = functools.partial(_ln_res_kernel, alpha=alpha)
    else:
        n_e = w_router.shape[-1]
        body = functools.partial(_ln_res_router_kernel, alpha=alpha)
        in_specs.append(pl.BlockSpec((d, n_e), lambda i: (0, 0)))
        args.append(w_router)
        pair = pl.BlockSpec((tm, TOP_K), lambda i: (i, 0))
        out_specs += [pair, pair]
        out_shape += [jax.ShapeDtypeStruct((m, TOP_K), I32), jax.ShapeDtypeStruct((m, TOP_K), F32)]
    return pl.pallas_call(
        body, grid=(pl.cdiv(m, tm),), in_specs=in_specs, out_specs=out_specs, out_shape=out_shape,
        compiler_params=_cparams(("arbitrary",), 12 * tm * d * 4 // MIB + 8),
        name="ln_res" if w_router is None else "ln_res_router",
    )(*args)


def _ln_res_combine(x, y1, y2, gates, g, b, alpha, tm):
    m, d = x.shape
    tm = min(tm, m)
    row = pl.BlockSpec((tm, d), lambda i: (i, 0))
    vec = pl.BlockSpec((1, d), lambda i: (0, 0))
    return pl.pallas_call(
        functools.partial(_ln_res_combine_kernel, alpha=alpha),
        grid=(pl.cdiv(m, tm),),
        in_specs=[row, row, row, pl.BlockSpec((tm, TOP_K), lambda i: (i, 0)), vec, vec],
        out_specs=[row, row],
        out_shape=[jax.ShapeDtypeStruct((m, d), F32), jax.ShapeDtypeStruct((m, d), BF16)],
        compiler_params=_cparams(("arbitrary",), 14 * tm * d * 4 // MIB + 8),
        name="ln_res_combine",
    )(x, y1, y2, gates, g.reshape(1, d), b.reshape(1, d))


def _is_new_expert(te_ref, t):
    prev = te_ref[jnp.maximum(t - 1, 0)]
    return jnp.logical_or(t == 0, te_ref[t] != prev)


def _moe_up_kernel(te_ref, nu_ref, x_ref, wg_ref, wu_ref, o_ref, wgb_ref, wub_ref):
    t = pl.program_id(1)

    @pl.when(_is_new_expert(te_ref, t))
    def _():
        wgb_ref[...] = wg_ref[...].astype(BF16)
        wub_ref[...] = wu_ref[...].astype(BF16)

    @pl.when(t < nu_ref[0])
    def _():
        x = x_ref[...]
        g = jnp.dot(x, wgb_ref[...], preferred_element_type=F32)
        u = jnp.dot(x, wub_ref[...], preferred_element_type=F32)
        o_ref[...] = (g * jax.nn.sigmoid(g) * u).astype(o_ref.dtype)


def _moe_down_kernel(te_ref, nu_ref, x_ref, w_ref, o_ref, wb_ref):
    t = pl.program_id(1)

    @pl.when(_is_new_expert(te_ref, t))
    def _():
        wb_ref[...] = w_ref[...].astype(BF16)

    @pl.when(t < nu_ref[0])
    def _():
        o_ref[...] = jnp.dot(x_ref[...], wb_ref[...], preferred_element_type=F32)


def _moe_up(xs, wg_stack, wu_stack, jl, tile_expert, n_used, tn):
    r, d = xs.shape
    n = wg_stack.shape[-1]
    tn = _pick_tile(n, tn)
    tm = MOE_TILE
    w_spec = pl.BlockSpec((None, None, d, tn), lambda j, t, te, nu: (jl, te[t], 0, j))
    vmem = (4 * d * tn * 4 + 2 * d * tn * 2 + 2 * tm * d * 2 + 5 * tm * tn * 4) // MIB + 6
    grid_spec = pltpu.PrefetchScalarGridSpec(
        num_scalar_prefetch=2,
        grid=(n // tn, r // tm),
        in_specs=[pl.BlockSpec((tm, d), lambda j, t, te, nu: (jnp.minimum(t, nu[0] - 1), 0)),
                  w_spec, w_spec],
        out_specs=pl.BlockSpec((tm, tn), lambda j, t, te, nu: (jnp.minimum(t, nu[0] - 1), j)),
        scratch_shapes=[pltpu.VMEM((d, tn), BF16), pltpu.VMEM((d, tn), BF16)],
    )
    return pl.pallas_call(
        _moe_up_kernel, grid_spec=grid_spec,
        out_shape=jax.ShapeDtypeStruct((r, n), BF16),
        compiler_params=_cparams(("arbitrary", "arbitrary"), vmem),
        name="moe_up",
    )(tile_expert, n_used, xs, wg_stack, wu_stack)


def _moe_down(hs, wd_stack, jl, tile_expert, n_used, tn):
    r, k = hs.shape
    n = wd_stack.shape[-1]
    tn = _pick_tile(n, tn)
    tm = MOE_TILE
    vmem = (2 * k * tn * 4 + k * tn * 2 + 2 * tm * k * 2 + 3 * tm * tn * 4) // MIB + 6
    grid_spec = pltpu.PrefetchScalarGridSpec(
        num_scalar_prefetch=2,
        grid=(n // tn, r // tm),
        in_specs=[pl.BlockSpec((tm, k), lambda j, t, te, nu: (jnp.minimum(t, nu[0] - 1), 0)),
                  pl.BlockSpec((None, None, k, tn), lambda j, t, te, nu: (jl, te[t], 0, j))],
        out_specs=pl.BlockSpec((tm, tn), lambda j, t, te, nu: (jnp.minimum(t, nu[0] - 1), j)),
        scratch_shapes=[pltpu.VMEM((k, tn), BF16)],
    )
    return pl.pallas_call(
        _moe_down_kernel, grid_spec=grid_spec,
        out_shape=jax.ShapeDtypeStruct((r, n), F32),
        compiler_params=_cparams(("arbitrary", "arbitrary"), vmem),
        name="moe_down",
    )(tile_expert, n_used, hs, wd_stack)


def _route(idx, n_exp, tile):
    n_tok = idx.shape[0]
    n_asg = n_tok * TOP_K
    nt_max = (n_asg + n_exp * (tile - 1)) // tile
    flat = idx.reshape(-1)
    onehot = (flat[:, None] == jnp.arange(n_exp, dtype=I32)[None, :]).astype(I32)
    csum = jnp.cumsum(onehot, axis=0)
    rank = jnp.take_along_axis(csum, flat[:, None], axis=1)[:, 0] - 1
    counts = csum[-1]
    ntile_e = (counts + tile - 1) // tile
    tile_end = jnp.cumsum(ntile_e)
    tile_start = tile_end - ntile_e
    dest = tile_start[flat] * tile + rank
    n_used = tile_end[-1]
    tile_ids = jnp.arange(nt_max, dtype=I32)
    te = jnp.sum((tile_ids[:, None] >= tile_end[None, :]).astype(I32), axis=1)
    te = jnp.minimum(te, n_exp - 1)
    te = jnp.where(tile_ids < n_used, te, te[n_used - 1])
    row_token = jnp.zeros((nt_max * tile,), I32).at[dest].set(jnp.arange(n_asg, dtype=I32) // TOP_K)
    return dest.reshape(n_tok, TOP_K), row_token, te.astype(I32), n_used.reshape(1).astype(I32)


def _sb_tile(q, k, v, bias, carry, acc, suffix_ext, mask):
    n_chunks = k.shape[0] // SB_BLOCK
    z = lax.dot_general(q, k, (((1,), (1,)), ((), ())), preferred_element_type=F32) * SB_SCALE + bias
    t = jnp.log(1.0 + jnp.exp(-jnp.abs(z)))
    log_rest = -(jnp.maximum(z, 0.0) + t)
    log_beta = jnp.minimum(z, 0.0) - t
    if mask is not None:
        log_rest = jnp.where(mask, log_rest, 0.0)
    hi = log_rest.astype(BF16)
    lo = (log_rest - hi.astype(F32)).astype(BF16)
    suffix = [None] * n_chunks
    for c in reversed(range(n_chunks)):
        cols = slice(c * SB_BLOCK, (c + 1) * SB_BLOCK)
        sums = (jnp.dot(hi[:, cols], suffix_ext, preferred_element_type=F32)
                + jnp.dot(lo[:, cols], suffix_ext, preferred_element_type=F32))
        suffix[c] = sums[:, :SB_BLOCK] + carry
        carry = carry + sums[:, SB_BLOCK:]
    a = jnp.exp(log_beta + jnp.concatenate(suffix, axis=1))
    if mask is not None:
        a = jnp.where(mask, a, 0.0)
    acc = acc + jnp.dot(a.astype(BF16), v, preferred_element_type=F32)
    return carry, acc


def _suffix_matrix_ext():
    r = lax.broadcasted_iota(I32, (SB_BLOCK, 2 * SB_BLOCK), 0)
    c = lax.broadcasted_iota(I32, (SB_BLOCK, 2 * SB_BLOCK), 1)
    return jnp.logical_or(r > c, c >= SB_BLOCK).astype(BF16)


def _attn_prompt_kernel(bias_ref, q_ref, k_ref, v_ref, o_ref, qb_ref, kb_ref, vb_ref, carry_ref, acc_ref,
                        *, n_tiles, tq):
    bias = bias_ref[pl.program_id(1)]
    qb_ref[...] = q_ref[...].astype(BF16)
    kb_ref[...] = k_ref[...].astype(BF16)
    vb_ref[...] = v_ref[...].astype(BF16)
    suffix_ext = _suffix_matrix_ext()
    r = lax.broadcasted_iota(I32, (tq, tq), 0)
    c = lax.broadcasted_iota(I32, (tq, tq), 1)
    diag_mask = c < r

    def q_tile(qi, _):
        q0 = pl.multiple_of(qi * tq, tq)
        q = qb_ref[pl.ds(q0, tq), :]
        carry, acc = _sb_tile(q, kb_ref[pl.ds(q0, tq), :], vb_ref[pl.ds(q0, tq), :], bias,
                              jnp.zeros((tq, SB_BLOCK), F32), jnp.zeros((tq, HEAD_DIM), F32),
                              suffix_ext, diag_mask)
        carry_ref[...] = carry
        acc_ref[...] = acc

        def k_tile(n, _):
            k0 = pl.multiple_of((qi - 1 - n) * tq, tq)
            carry, acc = _sb_tile(q, kb_ref[pl.ds(k0, tq), :], vb_ref[pl.ds(k0, tq), :], bias,
                                  carry_ref[...], acc_ref[...], suffix_ext, None)
            carry_ref[...] = carry
            acc_ref[...] = acc
            return 0

        lax.fori_loop(0, qi, k_tile, 0)
        o_ref[pl.ds(q0, tq), :] = acc_ref[...]
        return 0

    lax.fori_loop(0, n_tiles, q_tile, 0)


def _attn_prompt(proj, sb_bias, n_batch, seq, n_heads, col_q, col_k, col_v):
    tq = min(SB_PROMPT_TILE, seq)
    assert seq % tq == 0
    blk_spec = lambda col: pl.BlockSpec((seq, HEAD_DIM), lambda b, h: (b, col // HEAD_DIM + h))
    return pl.pallas_call(
        functools.partial(_attn_prompt_kernel, n_tiles=seq // tq, tq=tq),
        grid=(n_batch, n_heads),
        in_specs=[pl.BlockSpec(memory_space=pltpu.SMEM), blk_spec(col_q), blk_spec(col_k), blk_spec(col_v)],
        out_specs=pl.BlockSpec((seq, HEAD_DIM), lambda b, h: (b, h)),
        out_shape=jax.ShapeDtypeStruct((n_batch * seq, n_heads * HEAD_DIM), F32),
        scratch_shapes=[pltpu.VMEM((seq, HEAD_DIM), BF16)] * 3
                       + [pltpu.VMEM((tq, SB_BLOCK), F32), pltpu.VMEM((tq, HEAD_DIM), F32)],
        compiler_params=_cparams(("arbitrary", "arbitrary"), 40),
        name="attn_prompt",
    )(sb_bias, proj, proj, proj)


def _attn_sample_kernel(pt_ref, q_ref, kn_ref, vn_ref, bias_ref, *refs, n_heads, n_q, pages_per_step):
    k_refs = refs[:pages_per_step]
    v_refs = refs[pages_per_step:2 * pages_per_step]
    o_ref = refs[2 * pages_per_step]
    qbd_ref, acc_ref, carry_ref = refs[2 * pages_per_step + 1:]
    j = pl.program_id(1)
    rows = n_heads * n_q
    width = n_heads * HEAD_DIM
    page = SB_BLOCK
    suffix_ext = _suffix_matrix_ext()
    bias = bias_ref[...]
    row_head = lax.broadcasted_iota(I32, (rows, width), 0) // n_q
    col_head = lax.broadcasted_iota(I32, (rows, width), 1) // HEAD_DIM

    @pl.when(j == 0)
    def _():
        qbd_ref[...] = jnp.where(row_head == col_head, q_ref[...], 0.0).astype(BF16)
        r = lax.broadcasted_iota(I32, (rows, page), 0) % n_q
        c = lax.broadcasted_iota(I32, (rows, page), 1)
        carry, acc = _sb_tile(qbd_ref[...], kn_ref[...].astype(BF16), vn_ref[...].astype(BF16), bias,
                              jnp.zeros((rows, page), F32), jnp.zeros((rows, width), F32), suffix_ext,
                              c < r)
        carry_ref[...] = carry
        acc_ref[...] = acc

    def head_major(page_ref):
        return jnp.concatenate([page_ref[pl.ds(h, page, stride=n_heads), :].astype(BF16)
                                for h in range(n_heads)], axis=1)

    k = jnp.concatenate([head_major(k_refs[p]) for p in reversed(range(pages_per_step))], axis=0)
    v = jnp.concatenate([head_major(v_refs[p]) for p in reversed(range(pages_per_step))], axis=0)
    carry, acc = _sb_tile(qbd_ref[...], k, v, bias, carry_ref[...], acc_ref[...], suffix_ext, None)
    carry_ref[...] = carry
    acc_ref[...] = acc

    @pl.when(j == pl.num_programs(1) - 1)
    def _():
        own = jnp.where(row_head == col_head, acc, 0.0)
        out = own[0:n_q, :]
        for h in range(1, n_heads):
            out = out + own[h * n_q:(h + 1) * n_q, :]
        o_ref[...] = out


def _attn_sample(q, k_new, v_new, bias_rows, cache_k, cache_v, page_table, layer, n_heads):
    n_b, rows, width = q.shape
    n_q = rows // n_heads
    n_pages = page_table.shape[1]
    page = SB_BLOCK
    pps = SAMPLE_PAGES_PER_STEP
    assert n_pages % pps == 0
    n_steps = n_pages // pps

    def page_spec(p):
        return pl.BlockSpec((None, None, page * n_heads, HEAD_DIM),
                            lambda b, j, pt: (pt[b, n_pages - 1 - (j * pps + p)], layer, 0, 0))

    grid_spec = pltpu.PrefetchScalarGridSpec(
        num_scalar_prefetch=1,
        grid=(n_b, n_steps),
        in_specs=[pl.BlockSpec((None, rows, width), lambda b, j, pt: (b, 0, 0)),
                  pl.BlockSpec((None, page, width), lambda b, j, pt: (b, 0, 0)),
                  pl.BlockSpec((None, page, width), lambda b, j, pt: (b, 0, 0)),
                  pl.BlockSpec((rows, 1), lambda b, j, pt: (0, 0))]
                 + [page_spec(p) for p in range(pps)] * 1
                 + [page_spec(p) for p in range(pps)],
        out_specs=pl.BlockSpec((None, n_q, width), lambda b, j, pt: (b, 0, 0)),
        scratch_shapes=[pltpu.VMEM((rows, width), BF16), pltpu.VMEM((rows, width), F32),
                        pltpu.VMEM((rows, page), F32)],
    )
    return pl.pallas_call(
        functools.partial(_attn_sample_kernel, n_heads=n_heads, n_q=n_q, pages_per_step=pps),
        grid_spec=grid_spec,
        out_shape=jax.ShapeDtypeStruct((n_b, n_q, width), F32),
        compiler_params=_cparams(("arbitrary", "arbitrary"), 48),
        name="attn_sample",
    )(page_table, q, k_new, v_new, bias_rows, *([cache_k] * pps), *([cache_v] * pps))


def _spatial_gate(u_act, v_norm, w_of_head, b_of_head, keep, n_heads):
    outs = []
    for h in range(n_heads):
        w = jnp.where(keep, w_of_head(h), 0.0).astype(BF16)
        v_h = v_norm[:, h * HEAD_DIM:(h + 1) * HEAD_DIM].astype(BF16)
        outs.append(jnp.dot(w, v_h, preferred_element_type=F32) + b_of_head(h))
    return u_act * jnp.concatenate(outs, axis=1)


def _group_a_prompt_kernel(u_ref, v_ref, ws_ref, bs_ref, lg_ref, lb_ref, og_ref, o_ref, *, n_heads):
    u_act = _gelu(u_ref[...])
    v_norm = _layer_norm(_gelu(v_ref[...]), lg_ref[...], lb_ref[...])
    r = lax.broadcasted_iota(I32, (CHUNK, CHUNK), 0)
    c = lax.broadcasted_iota(I32, (CHUNK, CHUNK), 1)
    y = _spatial_gate(u_act, v_norm, lambda h: ws_ref[h], lambda h: bs_ref[h], c <= r, n_heads)
    o_ref[...] = _rms_norm(y, og_ref[...]).astype(BF16)


def _group_a_prompt(proj, w_s, b_s, lnv_g, lnv_b, og, n_rows, d_a):
    n_heads = d_a // HEAD_DIM
    vec = pl.BlockSpec((1, d_a), lambda c: (0, 0))
    return pl.pallas_call(
        functools.partial(_group_a_prompt_kernel, n_heads=n_heads),
        grid=(n_rows // CHUNK,),
        in_specs=[pl.BlockSpec((CHUNK, d_a), lambda c: (c, 0)),
                  pl.BlockSpec((CHUNK, d_a), lambda c: (c, 1)),
                  pl.BlockSpec((n_heads, CHUNK, CHUNK), lambda c: (0, 0, 0)),
                  pl.BlockSpec((n_heads, CHUNK, 1), lambda c: (0, 0, 0)),
                  vec, vec, vec],
        out_specs=pl.BlockSpec((CHUNK, d_a), lambda c: (c, 0)),
        out_shape=jax.ShapeDtypeStruct((n_rows, d_a), BF16),
        compiler_params=_cparams(("arbitrary",), 16),
        name="group_a_prompt",
    )(proj, proj, w_s, b_s.reshape(n_heads, CHUNK, 1), lnv_g.reshape(1, d_a), lnv_b.reshape(1, d_a),
      og.reshape(1, d_a))


CONV_HALO = 8


def _group_c_prompt_kernel(gb_ref, gc_ref, xc_ref, gch_ref, xch_ref, cw_ref, og_ref, o_ref, new_ref,
                           *, tc):
    t = pl.program_id(1)
    p = gc_ref[...] * xc_ref[...]
    halo = jnp.where(t > 0, gch_ref[...] * xch_ref[...], 0.0)
    ext = jnp.concatenate([halo, p], axis=0)
    p1 = pltpu.roll(ext, 1, 0)[CONV_HALO:, :]
    p2 = pltpu.roll(ext, 2, 0)[CONV_HALO:, :]
    cw = cw_ref[...]
    conv = cw[2:3, :] * p
    conv = conv + cw[0:1, :] * p2
    conv = conv + cw[1:2, :] * p1
    o_ref[...] = _rms_norm(gb_ref[...] * conv, og_ref[...]).astype(BF16)

    @pl.when(t == pl.num_programs(1) - 1)
    def _():
        new_ref[...] = p[tc - 2:tc, :]


def _group_c_prompt(proj, conv_w, og, n_batch, seq, d_c, col_gb):
    cb = col_gb // d_c
    tc = min(512, seq)
    n_t = seq // tc
    tile = lambda col: pl.BlockSpec((tc, d_c), lambda b, t: (b * n_t + t, col))
    halo = lambda col: pl.BlockSpec(
        (CONV_HALO, d_c), lambda b, t: (jnp.maximum((b * seq + t * tc) // CONV_HALO - 1, 0), col))
    return pl.pallas_call(
        functools.partial(_group_c_prompt_kernel, tc=tc),
        grid=(n_batch, n_t),
        in_specs=[tile(cb), tile(cb + 1), tile(cb + 2), halo(cb + 1), halo(cb + 2),
                  pl.BlockSpec((3, d_c), lambda b, t: (0, 0)), pl.BlockSpec((1, d_c), lambda b, t: (0, 0))],
        out_specs=[pl.BlockSpec((tc, d_c), lambda b, t: (b * n_t + t, 0)),
                   pl.BlockSpec((None, 2, d_c), lambda b, t: (b, 0, 0))],
        out_shape=[jax.ShapeDtypeStruct((n_batch * seq, d_c), BF16),
                   jax.ShapeDtypeStruct((n_batch, 2, d_c), F32)],
        compiler_params=_cparams(("arbitrary", "arbitrary"), 32),
        name="group_c_prompt",
    )(proj, proj, proj, proj, proj, conv_w, og.reshape(1, d_c))


def _mixer_cat_prompt_kernel(ya_ref, yb_ref, yc_ref, og_ref, o_ref, *, d_a, d_b):
    o_ref[:, 0:d_a] = ya_ref[...]
    o_ref[:, d_a:d_a + d_b] = _rms_norm(yb_ref[...], og_ref[...]).astype(BF16)
    o_ref[:, d_a + d_b:] = yc_ref[...]


def _mixer_cat_prompt(ya, yb, yc, og_b, tm):
    n_rows, d_a = ya.shape
    d_b = yb.shape[1]
    d_c = yc.shape[1]
    return pl.pallas_call(
        functools.partial(_mixer_cat_prompt_kernel, d_a=d_a, d_b=d_b),
        grid=(n_rows // tm,),
        in_specs=[pl.BlockSpec((tm, d_a), lambda i: (i, 0)), pl.BlockSpec((tm, d_b), lambda i: (i, 0)),
                  pl.BlockSpec((tm, d_c), lambda i: (i, 0)), pl.BlockSpec((1, d_b), lambda i: (0, 0))],
        out_specs=pl.BlockSpec((tm, d_a + d_b + d_c), lambda i: (i, 0)),
        out_shape=jax.ShapeDtypeStruct((n_rows, d_a + d_b + d_c), BF16),
        compiler_params=_cparams(("arbitrary",), 32),
        name="mixer_cat_prompt",
    )(ya, yb, yc, og_b.reshape(1, d_b))


def _mixer_sample_kernel(proj_ref, yb_ref, wt_ref, bt_ref, lg_ref, lb_ref, cw_ref, e1_ref, e2_ref,
                         og_ref, o_ref, va_ref, p_ref, *, n_q, d_a, d_b, d_c):
    n_rows = proj_ref.shape[0]
    n_heads_a = d_a // HEAD_DIM
    c_gb = 2 * d_a + 3 * d_b
    og = og_ref[...]
    u_act = _gelu(proj_ref[:, 0:d_a])
    v_norm = _layer_norm(_gelu(proj_ref[:, d_a:2 * d_a]), lg_ref[...], lb_ref[...])
    va_ref[...] = v_norm
    r = lax.broadcasted_iota(I32, (n_rows, n_rows), 0)
    c = lax.broadcasted_iota(I32, (n_rows, n_rows), 1)
    keep = jnp.logical_and(c <= r, r // n_q == c // n_q)
    ya = _spatial_gate(u_act, v_norm, lambda h: wt_ref[h], lambda h: bt_ref[h], keep, n_heads_a)
    o_ref[:, 0:d_a] = _rms_norm(ya, og[:, 0:d_a]).astype(BF16)
    o_ref[:, d_a:d_a + d_b] = _rms_norm(yb_ref[...], og[:, d_a:d_a + d_b]).astype(BF16)
    p = proj_ref[:, c_gb + d_c:c_gb + 2 * d_c] * proj_ref[:, c_gb + 2 * d_c:c_gb + 3 * d_c]
    p_ref[...] = p
    pos = lax.broadcasted_iota(I32, p.shape, 0) % n_q
    p1 = jnp.where(pos >= 1, pltpu.roll(p, 1, 0), e1_ref[...])
    p2 = jnp.where(pos >= 2, pltpu.roll(p, 2, 0), e2_ref[...])
    cw = cw_ref[...]
    conv = cw[2:3, :] * p
    conv = conv + cw[0:1, :] * p2
    conv = conv + cw[1:2, :] * p1
    yc = proj_ref[:, c_gb:c_gb + d_c] * conv
    o_ref[:, d_a + d_b:] = _rms_norm(yc, og[:, d_a + d_b:]).astype(BF16)


def _mixer_sample(proj, row0, yb, w_s, b_s, lnv_g, lnv_b, conv_w, state, og, n_b, n_q, d_a, d_b, d_c):
    n = n_b * n_q
    assert row0 % n == 0
    d_in = proj.shape[1]
    n_heads_a = d_a // HEAD_DIM
    d_mix = d_a + d_b + d_c
    wt = jnp.tile(w_s[:, :n_q, :n_q], (1, n_b, n_b))
    bt = jnp.tile(b_s[:, :n_q], (1, n_b)).reshape(n_heads_a, n, 1)
    e1 = jnp.repeat(state[:, 1:2, :], n_q, axis=1).reshape(n, d_c)
    e2 = jnp.tile(state, (1, n_q // 2, 1)).reshape(n, d_c)
    full = lambda shape: pl.BlockSpec(shape, lambda i: (0,) * len(shape))
    return pl.pallas_call(
        functools.partial(_mixer_sample_kernel, n_q=n_q, d_a=d_a, d_b=d_b, d_c=d_c),
        grid=(1,),
        in_specs=[pl.BlockSpec((n, d_in), lambda i: (row0 // n, 0)), full((n, d_b)),
                  full((n_heads_a, n, n)), full((n_heads_a, n, 1)), full((1, d_a)), full((1, d_a)),
                  full((3, d_c)), full((n, d_c)), full((n, d_c)), full((1, d_mix))],
        out_specs=[full((n, d_mix)), full((n, d_a)), full((n, d_c))],
        out_shape=[jax.ShapeDtypeStruct((n, d_mix), BF16), jax.ShapeDtypeStruct((n, d_a), F32),
                   jax.ShapeDtypeStruct((n, d_c), F32)],
        compiler_params=_cparams(("arbitrary",), 16),
        name="mixer_sample",
    )(proj, yb, wt, bt, lnv_g.reshape(1, d_a), lnv_b.reshape(1, d_a), conv_w, e1, e2,
      og.reshape(1, d_mix))


def kernel(x_prompt, x_sample, cache_k, cache_v, state_conv, page_table, p_prompt, p_sample, w_in, w_spatial, b_spatial, lnv_g, lnv_b, conv_w, sb_bias, out_norm_g, w_out, ln1_g, ln1_b, ln2_g, ln2_b, w_ffn_gate, w_ffn_up, w_ffn_down, w_router, w_exp_gate, w_exp_up, w_exp_down, w_ple_gate, w_ple_proj):
    n_b, seq, d = x_prompt.shape
    n_db, n_q, _ = x_sample.shape
    depth = w_in.shape[0]
    n_heads_b = cache_k.shape[3]
    page = cache_k.shape[2]
    assert page == SB_BLOCK and cache_k.shape[4] == HEAD_DIM and seq % SB_BLOCK == 0
    d_a = lnv_g.shape[1]
    d_b = n_heads_b * HEAD_DIM
    d_c = conv_w.shape[2]
    n_exp = w_router.shape[-1]
    n_p = n_b * seq
    n_s = n_db * n_q
    alpha = float((2 * depth) ** 0.25)
    col_q, col_k, col_v = 2 * d_a, 2 * d_a + d_b, 2 * d_a + 2 * d_b
    col_gb = 2 * d_a + 3 * d_b

    xf = jnp.concatenate([x_prompt.reshape(n_p, d), x_sample.reshape(n_s, d)], axis=0)
    xb = xf.astype(BF16)
    ck = cache_k.reshape(cache_k.shape[0], depth, page * n_heads_b, HEAD_DIM)
    cv = cache_v.reshape(cache_v.shape[0], depth, page * n_heads_b, HEAD_DIM)
    page_table = page_table.astype(I32)

    kp, vp, convp, ks, vs, convs, vas = [], [], [], [], [], [], []
    for i in range(depth):
        og = out_norm_g[i]
        proj = _mm(xb, w_in, i, F32, 512, 1408, "proj_in")
        ya = _group_a_prompt(proj, w_spatial[i], b_spatial[i], lnv_g[i], lnv_b[i], og[:d_a], n_p, d_a)
        yb = _attn_prompt(proj, sb_bias[i], n_b, seq, n_heads_b, col_q, col_k, col_v)
        yc, conv_new_p = _group_c_prompt(proj, conv_w[i], og[d_a + d_b:], n_b, seq, d_c, col_gb)
        ycat_p = _mixer_cat_prompt(ya, yb, yc, og[d_a:d_a + d_b], 512)
        proj_s = proj[n_p:]
        q_s = proj_s[:, col_q:col_k].reshape(n_db, n_q, d_b)
        k_s = proj_s[:, col_k:col_v].reshape(n_db, n_q, d_b)
        v_s = proj_s[:, col_v:col_gb].reshape(n_db, n_q, d_b)
        pad = ((0, 0), (0, page - n_q), (0, 0))
        bias_rows = jnp.repeat(sb_bias[i], n_q).reshape(n_heads_b * n_q, 1)
        yb_s = _attn_sample(jnp.tile(q_s, (1, n_heads_b, 1)), jnp.pad(k_s, pad), jnp.pad(v_s, pad),
                            bias_rows, ck, cv, page_table, i, n_heads_b)
        ycat_s, va_s, pc_s = _mixer_sample(proj, n_p, yb_s.reshape(n_s, d_b), w_spatial[i], b_spatial[i],
                                           lnv_g[i], lnv_b[i], conv_w[i], state_conv[:, i], og,
                                           n_db, n_q, d_a, d_b, d_c)
        ycat = jnp.concatenate([ycat_p, ycat_s], axis=0)
        kp.append(proj[:n_p, col_k:col_v].reshape(n_b, seq, n_heads_b, HEAD_DIM))
        vp.append(proj[:n_p, col_v:col_gb].reshape(n_b, seq, n_heads_b, HEAD_DIM))
        convp.append(conv_new_p)
        ks.append(k_s.reshape(n_db, n_q, n_heads_b, HEAD_DIM))
        vs.append(v_s.reshape(n_db, n_q, n_heads_b, HEAD_DIM))
        convs.append(pc_s.reshape(n_db, n_q, d_c)[:, n_q - 2:])
        vas.append(va_s.reshape(n_db, n_q, d_a))
        h = _mm(ycat, w_out, i, F32, 1024, 1024, "proj_out")
        jl = i // 2
        if i % 2 == 0:
            x1f, x1b = _ln_res(xf, h, ln1_g[i], ln1_b[i], alpha, 256)
            hid = _swiglu_up(x1b, w_ffn_gate, w_ffn_up, jl, 1024, 512)
            f = _mm(hid, w_ffn_down, jl, F32, 512, 512, "ffn_down")
            x2f, x2b = _ln_res(x1f, f, ln2_g[i], ln2_b[i], alpha, 256)
        else:
            x1f, x1b, idx, gates = _ln_res(xf, h, ln1_g[i], ln1_b[i], alpha, 256, w_router=w_router[jl])
            dest, row_token, tile_expert, n_used = _route(idx, n_exp, MOE_TILE)
            xs = jnp.take(x1b, row_token, axis=0)
            hid = _moe_up(xs, w_exp_gate, w_exp_up, jl, tile_expert, n_used, 1024)
            ys = _moe_down(hid, w_exp_down, jl, tile_expert, n_used, 512)
            y1 = jnp.take(ys, dest[:, 0], axis=0)
            y2 = jnp.take(ys, dest[:, 1], axis=0)
            x2f, x2b = _ln_res_combine(x1f, y1, y2, gates, ln2_g[i], ln2_b[i], alpha, 256)
        p_i = jnp.concatenate([p_prompt[i].reshape(n_p, -1), p_sample[i].reshape(n_s, -1)], axis=0)
        xf, xb = _ple(x2b, x2f, p_i, w_ple_gate, w_ple_proj, i, 512, 1024)

    return (xf[:n_p].reshape(n_b, seq, d), xf[n_p:].reshape(n_db, n_q, d),
            jnp.stack(kp, axis=1), jnp.stack(vp, axis=1), jnp.stack(convp, axis=1),
            jnp.stack(ks, axis=1), jnp.stack(vs, axis=1), jnp.stack(convs, axis=1),
            jnp.stack(vas, axis=1))
```

```python
import functools

import jax
import jax.numpy as jnp
from jax import lax
from jax.experimental import pallas as pl
from jax.experimental.pallas import tpu as pltpu

F32 = jnp.float32
BF16 = jnp.bfloat16
I32 = jnp.int32

HEAD_DIM = 128
CHUNK = 128
TOP_K = 2
LN_EPS = 1e-5
RMS_EPS = 1e-6
SB_BLOCK = 128
SB_PROMPT_TILE = 512
SB_SCALE = HEAD_DIM ** -0.5
LANE = 128
MIB = 1024 * 1024
MOE_TILE = 256
SAMPLE_PAGES_PER_STEP = 8


def _cparams(semantics, vmem_mib):
    return pltpu.CompilerParams(dimension_semantics=semantics, vmem_limit_bytes=vmem_mib * MIB)


def _pick_tile(n, pref, align=LANE):
    if n <= pref:
        return n
    best = None
    for t in range(align, pref + 1, align):
        if n % t == 0:
            best = t
    assert best is not None, (n, pref)
    return best


def _row_tile_dispatch(i, n_tiles, tm, m_total, fn):
    rem = m_total - (n_tiles - 1) * tm
    if rem == tm:
        fn(tm)
    else:
        pl.when(i < n_tiles - 1)(lambda: fn(tm))
        pl.when(i == n_tiles - 1)(lambda: fn(rem))


def _layer_norm(xf, g, b):
    mu = jnp.mean(xf, axis=-1, keepdims=True)
    xc = xf - mu
    var = jnp.mean(xc * xc, axis=-1, keepdims=True)
    return xc * lax.rsqrt(var + LN_EPS) * g + b


def _rms_norm(xf, g):
    ms = jnp.mean(xf * xf, axis=-1, keepdims=True)
    return xf * lax.rsqrt(ms + RMS_EPS) * g


def _gelu(x):
    return x * (lax.erf(x / (2.0 ** 0.5)) + 1.0) / 2.0


def _mm_kernel(x_ref, w_ref, o_ref, wb_ref, *, tm, m_total, n_tiles):
    i = pl.program_id(1)

    @pl.when(i == 0)
    def _():
        wb_ref[...] = w_ref[...].astype(BF16)

    def fn(rows):
        acc = jnp.dot(x_ref[0:rows, :], wb_ref[...], preferred_element_type=F32)
        o_ref[0:rows, :] = acc.astype(o_ref.dtype)

    _row_tile_dispatch(i, n_tiles, tm, m_total, fn)


def _mm(x, w_stack, layer, out_dtype, tm, tn, name):
    m, k = x.shape
    n = w_stack.shape[-1]
    tm = min(tm, m)
    tn = _pick_tile(n, tn)
    n_tiles = pl.cdiv(m, tm)
    vmem = (2 * k * tn * 4 + k * tn * 2 + 2 * tm * k * 2 + 3 * tm * tn * 4) // MIB + 6
    return pl.pallas_call(
        functools.partial(_mm_kernel, tm=tm, m_total=m, n_tiles=n_tiles),
        grid=(n // tn, n_tiles),
        in_specs=[pl.BlockSpec((tm, k), lambda j, i: (i, 0)),
                  pl.BlockSpec((None, k, tn), lambda j, i: (layer, 0, j))],
        out_specs=pl.BlockSpec((tm, tn), lambda j, i: (i, j)),
        out_shape=jax.ShapeDtypeStruct((m, n), out_dtype),
        scratch_shapes=[pltpu.VMEM((k, tn), BF16)],
        compiler_params=_cparams(("arbitrary", "arbitrary"), vmem),
        name=name,
    )(x, w_stack)


def _swiglu_up_kernel(x_ref, wg_ref, wu_ref, o_ref, wgb_ref, wub_ref, *, tm, m_total, n_tiles):
    i = pl.program_id(1)

    @pl.when(i == 0)
    def _():
        wgb_ref[...] = wg_ref[...].astype(BF16)
        wub_ref[...] = wu_ref[...].astype(BF16)

    def fn(rows):
        x = x_ref[0:rows, :]
        g = jnp.dot(x, wgb_ref[...], preferred_element_type=F32)
        u = jnp.dot(x, wub_ref[...], preferred_element_type=F32)
        o_ref[0:rows, :] = (g * jax.nn.sigmoid(g) * u).astype(o_ref.dtype)

    _row_tile_dispatch(i, n_tiles, tm, m_total, fn)


def _swiglu_up(x, wg_stack, wu_stack, layer, tm, tn):
    m, k = x.shape
    n = wg_stack.shape[-1]
    tm = min(tm, m)
    tn = _pick_tile(n, tn)
    n_tiles = pl.cdiv(m, tm)
    vmem = (4 * k * tn * 4 + 2 * k * tn * 2 + 2 * tm * k * 2 + 5 * tm * tn * 4) // MIB + 6
    w_spec = pl.BlockSpec((None, k, tn), lambda j, i: (layer, 0, j))
    return pl.pallas_call(
        functools.partial(_swiglu_up_kernel, tm=tm, m_total=m, n_tiles=n_tiles),
        grid=(n // tn, n_tiles),
        in_specs=[pl.BlockSpec((tm, k), lambda j, i: (i, 0)), w_spec, w_spec],
        out_specs=pl.BlockSpec((tm, tn), lambda j, i: (i, j)),
        out_shape=jax.ShapeDtypeStruct((m, n), BF16),
        scratch_shapes=[pltpu.VMEM((k, tn), BF16), pltpu.VMEM((k, tn), BF16)],
        compiler_params=_cparams(("arbitrary", "arbitrary"), vmem),
        name="ffn_up",
    )(x, wg_stack, wu_stack)


def _ple_kernel(xb_ref, xf_ref, p_ref, wg_ref, wp_ref, of_ref, ob_ref, wgb_ref, wpb_ref,
                *, tm, m_total, n_tiles):
    i = pl.program_id(1)

    @pl.when(i == 0)
    def _():
        wgb_ref[...] = wg_ref[...].astype(BF16)
        wpb_ref[...] = wp_ref[...].astype(BF16)

    def fn(rows):
        gate = jnp.dot(xb_ref[0:rows, :], wgb_ref[...], preferred_element_type=F32)
        proj = jnp.dot(p_ref[0:rows, :].astype(BF16), wpb_ref[...], preferred_element_type=F32)
        y = xf_ref[0:rows, :] + jax.nn.sigmoid(gate) * proj
        of_ref[0:rows, :] = y
        ob_ref[0:rows, :] = y.astype(BF16)

    _row_tile_dispatch(i, n_tiles, tm, m_total, fn)


def _ple(xb, xf, p, wg_stack, wp_stack, layer, tm, tn):
    m, d = xf.shape
    pd = p.shape[1]
    tm = min(tm, m)
    tn = _pick_tile(d, tn)
    n_tiles = pl.cdiv(m, tm)
    vmem = (2 * d * tn * 4 + d * tn * 2 + 3 * pd * tn * 4 + 2 * tm * d * 2 + 2 * tm * pd * 4
            + 9 * tm * tn * 4) // MIB + 6
    return pl.pallas_call(
        functools.partial(_ple_kernel, tm=tm, m_total=m, n_tiles=n_tiles),
        grid=(d // tn, n_tiles),
        in_specs=[pl.BlockSpec((tm, d), lambda j, i: (i, 0)),
                  pl.BlockSpec((tm, tn), lambda j, i: (i, j)),
                  pl.BlockSpec((tm, pd), lambda j, i: (i, 0)),
                  pl.BlockSpec((None, d, tn), lambda j, i: (layer, 0, j)),
                  pl.BlockSpec((None, pd, tn), lambda j, i: (layer, 0, j))],
        out_specs=[pl.BlockSpec((tm, tn), lambda j, i: (i, j)),
                   pl.BlockSpec((tm, tn), lambda j, i: (i, j))],
        out_shape=[jax.ShapeDtypeStruct((m, d), F32), jax.ShapeDtypeStruct((m, d), BF16)],
        scratch_shapes=[pltpu.VMEM((d, tn), BF16), pltpu.VMEM((pd, tn), BF16)],
        compiler_params=_cparams(("arbitrary", "arbitrary"), vmem),
        name="ple",
    )(xb, xf, p, wg_stack, wp_stack)


def _top2(logits):
    n_e = logits.shape[-1]
    lane = lax.broadcasted_iota(I32, logits.shape, 1)
    m1 = jnp.max(logits, axis=-1, keepdims=True)
    i1 = jnp.min(jnp.where(logits == m1, lane, n_e), axis=-1, keepdims=True)
    rest = jnp.where(lane == i1, -jnp.inf, logits)
    m2 = jnp.max(rest, axis=-1, keepdims=True)
    i2 = jnp.min(jnp.where(rest == m2, lane, n_e), axis=-1, keepdims=True)
    e2 = jnp.exp(m2 - m1)
    g1 = 1.0 / (1.0 + e2)
    g2 = e2 / (1.0 + e2)
    slot = lax.broadcasted_iota(I32, (logits.shape[0], TOP_K), 1)
    return jnp.where(slot == 0, i1, i2), jnp.where(slot == 0, g1, g2)


def _ln_res_kernel(x_ref, h_ref, g_ref, b_ref, of_ref, ob_ref, *, alpha):
    y = _layer_norm(alpha * x_ref[...] + h_ref[...], g_ref[...], b_ref[...])
    of_ref[...] = y
    ob_ref[...] = y.astype(BF16)


def _ln_res_router_kernel(x_ref, h_ref, g_ref, b_ref, wr_ref, of_ref, ob_ref, idx_ref, gate_ref,
                          *, alpha):
    y = _layer_norm(alpha * x_ref[...] + h_ref[...], g_ref[...], b_ref[...])
    of_ref[...] = y
    ob_ref[...] = y.astype(BF16)
    logits = jnp.dot(y.astype(BF16), wr_ref[...].astype(BF16), preferred_element_type=F32)
    idx, gates = _top2(logits)
    idx_ref[...] = idx
    gate_ref[...] = gates


def _ln_res(x, h, g, b, alpha, tm, w_router=None):
    m, d = x.shape
    tm = min(tm, m)
    row = pl.BlockSpec((tm, d), lambda i: (i, 0))
    vec = pl.BlockSpec((1, d), lambda i: (0, 0))
    out_specs = [row, row]
    out_shape = [jax.ShapeDtypeStruct((m, d), F32), jax.ShapeDtypeStruct((m, d), BF16)]
    in_specs = [row, row, vec, vec]
    args = [x, h, g.reshape(1, d), b.reshape(1, d)]
    if w_router is None:
        body = functools.partial(_ln_res_kernel, alpha=alpha)
    else:
        n_e = w_router.shape[-1]
        body = functools.partial(_ln_res_router_kernel, alpha=alpha)
        in_specs.append(pl.BlockSpec((d, n_e), lambda i: (0, 0)))
        args.append(w_router)
        pair = pl.BlockSpec((tm, TOP_K), lambda i: (i, 0))
        out_specs += [pair, pair]
        out_shape += [jax.ShapeDtypeStruct((m, TOP_K), I32), jax.ShapeDtypeStruct((m, TOP_K), F32)]
    return pl.pallas_call(
        body, grid=(pl.cdiv(m, tm),), in_specs=in_specs, out_specs=out_specs, out_shape=out_shape,
        compiler_params=_cparams(("arbitrary",), 12 * tm * d * 4 // MIB + 8),
        name="ln_res" if w_router is None else "ln_res_router",
    )(*args)


def _ln_res_combine_kernel(dest_ref, x_ref, gate_ref, g_ref, b_ref, ys_ref, of_ref, ob_ref, ybuf_ref,
                           sem_ref, *, alpha, tm, n_tok):
    base = pl.program_id(0) * tm

    def row_copy(r, slot):
        tok = jnp.minimum(base + r, n_tok - 1)
        src = dest_ref[tok * TOP_K + slot]
        return pltpu.make_async_copy(ys_ref.at[pl.ds(src, 1), :], ybuf_ref.at[slot, pl.ds(r, 1), :],
                                     sem_ref.at[slot])

    def start(r, carry):
        for slot in range(TOP_K):
            row_copy(r, slot).start()
        return carry

    def wait(r, carry):
        for slot in range(TOP_K):
            row_copy(r, slot).wait()
        return carry

    lax.fori_loop(0, tm, start, 0)
    lax.fori_loop(0, tm, wait, 0)
    gates = gate_ref[...]
    f = gates[:, 0:1] * ybuf_ref[0] + gates[:, 1:2] * ybuf_ref[1]
    y = _layer_norm(alpha * x_ref[...] + f, g_ref[...], b_ref[...])
    of_ref[...] = y
    ob_ref[...] = y.astype(BF16)


def _ln_res_combine(x, ys, dest, gates, g, b, alpha, tm):
    m, d = x.shape
    tm = min(tm, m)
    row = pl.BlockSpec((tm, d), lambda i, dst: (i, 0))
    vec = pl.BlockSpec((1, d), lambda i, dst: (0, 0))
    grid_spec = pltpu.PrefetchScalarGridSpec(
        num_scalar_prefetch=1,
        grid=(pl.cdiv(m, tm),),
        in_specs=[row, pl.BlockSpec((tm, TOP_K), lambda i, dst: (i, 0)), vec, vec,
                  pl.BlockSpec(memory_space=pl.ANY)],
        out_specs=[row, pl.BlockSpec((tm, d), lambda i, dst: (i, 0))],
        scratch_shapes=[pltpu.VMEM((TOP_K, tm, d), F32), pltpu.SemaphoreType.DMA((TOP_K,))],
    )
    return pl.pallas_call(
        functools.partial(_ln_res_combine_kernel, alpha=alpha, tm=tm, n_tok=m),
        grid_spec=grid_spec,
        out_shape=[jax.ShapeDtypeStruct((m, d), F32), jax.ShapeDtypeStruct((m, d), BF16)],
        compiler_params=_cparams(("arbitrary",), 14 * tm * d * 4 // MIB + 8),
        name="ln_res_combine",
    )(dest.reshape(-1), x, gates, g.reshape(1, d), b.reshape(1, d), ys)


def _is_new_expert(te_ref, t):
    prev = te_ref[jnp.maximum(t - 1, 0)]
    return jnp.logical_or(t == 0, te_ref[t] != prev)


def _moe_up_kernel(te_ref, nu_ref, x_ref, wg_ref, wu_ref, o_ref, wgb_ref, wub_ref):
    t = pl.program_id(1)

    @pl.when(_is_new_expert(te_ref, t))
    def _():
        wgb_ref[...] = wg_ref[...].astype(BF16)
        wub_ref[...] = wu_ref[...].astype(BF16)

    @pl.when(t < nu_ref[0])
    def _():
        x = x_ref[...]
        g = jnp.dot(x, wgb_ref[...], preferred_element_type=F32)
        u = jnp.dot(x, wub_ref[...], preferred_element_type=F32)
        o_ref[...] = (g * jax.nn.sigmoid(g) * u).astype(o_ref.dtype)


def _moe_down_kernel(te_ref, nu_ref, x_ref, w_ref, o_ref, wb_ref):
    t = pl.program_id(1)

    @pl.when(_is_new_expert(te_ref, t))
    def _():
        wb_ref[...] = w_ref[...].astype(BF16)

    @pl.when(t < nu_ref[0])
    def _():
        o_ref[...] = jnp.dot(x_ref[...], wb_ref[...], preferred_element_type=F32)


def _moe_up(xs, wg_stack, wu_stack, jl, tile_expert, n_used, tn):
    r, d = xs.shape
    n = wg_stack.shape[-1]
    tn = _pick_tile(n, tn)
    tm = MOE_TILE
    w_spec = pl.BlockSpec((None, None, d, tn), lambda j, t, te, nu: (jl, te[t], 0, j))
    vmem = (4 * d * tn * 4 + 2 * d * tn * 2 + 2 * tm * d * 2 + 5 * tm * tn * 4) // MIB + 6
    grid_spec = pltpu.PrefetchScalarGridSpec(
        num_scalar_prefetch=2,
        grid=(n // tn, r // tm),
        in_specs=[pl.BlockSpec((tm, d), lambda j, t, te, nu: (jnp.minimum(t, nu[0] - 1), 0)),
                  w_spec, w_spec],
        out_specs=pl.BlockSpec((tm, tn), lambda j, t, te, nu: (jnp.minimum(t, nu[0] - 1), j)),
        scratch_shapes=[pltpu.VMEM((d, tn), BF16), pltpu.VMEM((d, tn), BF16)],
    )
    return pl.pallas_call(
        _moe_up_kernel, grid_spec=grid_spec,
        out_shape=jax.ShapeDtypeStruct((r, n), BF16),
        compiler_params=_cparams(("arbitrary", "arbitrary"), vmem),
        name="moe_up",
    )(tile_expert, n_used, xs, wg_stack, wu_stack)


def _moe_down(hs, wd_stack, jl, tile_expert, n_used, tn):
    r, k = hs.shape
    n = wd_stack.shape[-1]
    tn = _pick_tile(n, tn)
    tm = MOE_TILE
    vmem = (2 * k * tn * 4 + k * tn * 2 + 2 * tm * k * 2 + 3 * tm * tn * 4) // MIB + 6
    grid_spec = pltpu.PrefetchScalarGridSpec(
        num_scalar_prefetch=2,
        grid=(n // tn, r // tm),
        in_specs=[pl.BlockSpec((tm, k), lambda j, t, te, nu: (jnp.minimum(t, nu[0] - 1), 0)),
                  pl.BlockSpec((None, None, k, tn), lambda j, t, te, nu: (jl, te[t], 0, j))],
        out_specs=pl.BlockSpec((tm, tn), lambda j, t, te, nu: (jnp.minimum(t, nu[0] - 1), j)),
        scratch_shapes=[pltpu.VMEM((k, tn), BF16)],
    )
    return pl.pallas_call(
        _moe_down_kernel, grid_spec=grid_spec,
        out_shape=jax.ShapeDtypeStruct((r, n), F32),
        compiler_params=_cparams(("arbitrary", "arbitrary"), vmem),
        name="moe_down",
    )(tile_expert, n_used, hs, wd_stack)


def _route(idx, n_exp, tile):
    n_tok = idx.shape[0]
    n_asg = n_tok * TOP_K
    nt_max = (n_asg + n_exp * (tile - 1)) // tile
    flat = idx.reshape(-1)
    onehot = (flat[:, None] == jnp.arange(n_exp, dtype=I32)[None, :]).astype(I32)
    csum = jnp.cumsum(onehot, axis=0)
    rank = jnp.take_along_axis(csum, flat[:, None], axis=1)[:, 0] - 1
    counts = csum[-1]
    ntile_e = (counts + tile - 1) // tile
    tile_end = jnp.cumsum(ntile_e)
    tile_start = tile_end - ntile_e
    dest = tile_start[flat] * tile + rank
    n_used = tile_end[-1]
    tile_ids = jnp.arange(nt_max, dtype=I32)
    te = jnp.sum((tile_ids[:, None] >= tile_end[None, :]).astype(I32), axis=1)
    te = jnp.minimum(te, n_exp - 1)
    te = jnp.where(tile_ids < n_used, te, te[n_used - 1])
    row_token = jnp.zeros((nt_max * tile,), I32).at[dest].set(jnp.arange(n_asg, dtype=I32) // TOP_K)
    return dest.reshape(n_tok, TOP_K), row_token, te.astype(I32), n_used.reshape(1).astype(I32)


def _sb_tile(q, k, v, bias, carry, acc, suffix_ext, mask):
    n_chunks = k.shape[0] // SB_BLOCK
    z = lax.dot_general(q, k, (((1,), (1,)), ((), ())), preferred_element_type=F32) * SB_SCALE + bias
    t = jnp.log(1.0 + jnp.exp(-jnp.abs(z)))
    log_rest = -(jnp.maximum(z, 0.0) + t)
    log_beta = jnp.minimum(z, 0.0) - t
    if mask is not None:
        log_rest = jnp.where(mask, log_rest, 0.0)
    hi = log_rest.astype(BF16)
    lo = (log_rest - hi.astype(F32)).astype(BF16)
    suffix = [None] * n_chunks
    for c in reversed(range(n_chunks)):
        cols = slice(c * SB_BLOCK, (c + 1) * SB_BLOCK)
        sums = jnp.dot(jnp.concatenate([hi[:, cols], lo[:, cols]], axis=1), suffix_ext,
                       preferred_element_type=F32)
        suffix[c] = sums[:, :SB_BLOCK] + carry
        carry = carry + sums[:, SB_BLOCK:]
    a = jnp.exp(log_beta + jnp.concatenate(suffix, axis=1))
    if mask is not None:
        a = jnp.where(mask, a, 0.0)
    acc = acc + jnp.dot(a.astype(BF16), v, preferred_element_type=F32)
    return carry, acc


def _suffix_matrix_ext():
    r = lax.broadcasted_iota(I32, (2 * SB_BLOCK, 2 * SB_BLOCK), 0) % SB_BLOCK
    c = lax.broadcasted_iota(I32, (2 * SB_BLOCK, 2 * SB_BLOCK), 1)
    return jnp.logical_or(r > c, c >= SB_BLOCK).astype(BF16)


def _attn_prompt_kernel(bias_ref, q_ref, k_ref, v_ref, o_ref, qb_ref, kb_ref, vb_ref, carry_ref, acc_ref,
                        *, n_tiles, tq):
    bias = bias_ref[pl.program_id(1)]
    qb_ref[...] = q_ref[...].astype(BF16)
    kb_ref[...] = k_ref[...].astype(BF16)
    vb_ref[...] = v_ref[...].astype(BF16)
    suffix_ext = _suffix_matrix_ext()
    r = lax.broadcasted_iota(I32, (tq, tq), 0)
    c = lax.broadcasted_iota(I32, (tq, tq), 1)
    diag_mask = c < r

    def q_tile(qi, _):
        q0 = pl.multiple_of(qi * tq, tq)
        q = qb_ref[pl.ds(q0, tq), :]
        carry, acc = _sb_tile(q, kb_ref[pl.ds(q0, tq), :], vb_ref[pl.ds(q0, tq), :], bias,
                              jnp.zeros((tq, SB_BLOCK), F32), jnp.zeros((tq, HEAD_DIM), F32),
                              suffix_ext, diag_mask)
        carry_ref[...] = carry
        acc_ref[...] = acc

        def k_tile(n, _):
            k0 = pl.multiple_of((qi - 1 - n) * tq, tq)
            carry, acc = _sb_tile(q, kb_ref[pl.ds(k0, tq), :], vb_ref[pl.ds(k0, tq), :], bias,
                                  carry_ref[...], acc_ref[...], suffix_ext, None)
            carry_ref[...] = carry
            acc_ref[...] = acc
            return 0

        lax.fori_loop(0, qi, k_tile, 0)
        o_ref[pl.ds(q0, tq), :] = acc_ref[...]
        return 0

    lax.fori_loop(0, n_tiles, q_tile, 0)


def _attn_prompt(proj, sb_bias, n_batch, seq, n_heads, col_q, col_k, col_v):
    tq = min(SB_PROMPT_TILE, seq)
    assert seq % tq == 0
    blk_spec = lambda col: pl.BlockSpec((seq, HEAD_DIM), lambda b, h: (b, col // HEAD_DIM + h))
    return pl.pallas_call(
        functools.partial(_attn_prompt_kernel, n_tiles=seq // tq, tq=tq),
        grid=(n_batch, n_heads),
        in_specs=[pl.BlockSpec(memory_space=pltpu.SMEM), blk_spec(col_q), blk_spec(col_k), blk_spec(col_v)],
        out_specs=pl.BlockSpec((seq, HEAD_DIM), lambda b, h: (b, h)),
        out_shape=jax.ShapeDtypeStruct((n_batch * seq, n_heads * HEAD_DIM), F32),
        scratch_shapes=[pltpu.VMEM((seq, HEAD_DIM), BF16)] * 3
                       + [pltpu.VMEM((tq, SB_BLOCK), F32), pltpu.VMEM((tq, HEAD_DIM), F32)],
        compiler_params=_cparams(("arbitrary", "arbitrary"), 40),
        name="attn_prompt",
    )(sb_bias, proj, proj, proj)


def _attn_sample_kernel(pt_ref, q_ref, kn_ref, vn_ref, bias_ref, *refs, n_heads, n_q, pages_per_step):
    k_refs = refs[:pages_per_step]
    v_refs = refs[pages_per_step:2 * pages_per_step]
    o_ref = refs[2 * pages_per_step]
    qbd_ref, acc_ref, carry_ref = refs[2 * pages_per_step + 1:]
    j = pl.program_id(1)
    rows = n_heads * n_q
    width = n_heads * HEAD_DIM
    page = SB_BLOCK
    suffix_ext = _suffix_matrix_ext()
    bias = bias_ref[...]
    row_head = lax.broadcasted_iota(I32, (rows, width), 0) // n_q
    col_head = lax.broadcasted_iota(I32, (rows, width), 1) // HEAD_DIM

    @pl.when(j == 0)
    def _():
        qbd_ref[...] = jnp.where(row_head == col_head, q_ref[...], 0.0).astype(BF16)
        r = lax.broadcasted_iota(I32, (rows, page), 0) % n_q
        c = lax.broadcasted_iota(I32, (rows, page), 1)
        carry, acc = _sb_tile(qbd_ref[...], kn_ref[...].astype(BF16), vn_ref[...].astype(BF16), bias,
                              jnp.zeros((rows, page), F32), jnp.zeros((rows, width), F32), suffix_ext,
                              c < r)
        carry_ref[...] = carry
        acc_ref[...] = acc

    def head_major(page_ref):
        return jnp.concatenate([page_ref[pl.ds(h, page, stride=n_heads), :].astype(BF16)
                                for h in range(n_heads)], axis=1)

    k = jnp.concatenate([head_major(k_refs[p]) for p in reversed(range(pages_per_step))], axis=0)
    v = jnp.concatenate([head_major(v_refs[p]) for p in reversed(range(pages_per_step))], axis=0)
    carry, acc = _sb_tile(qbd_ref[...], k, v, bias, carry_ref[...], acc_ref[...], suffix_ext, None)
    carry_ref[...] = carry
    acc_ref[...] = acc

    @pl.when(j == pl.num_programs(1) - 1)
    def _():
        own = jnp.where(row_head == col_head, acc, 0.0)
        out = own[0:n_q, :]
        for h in range(1, n_heads):
            out = out + own[h * n_q:(h + 1) * n_q, :]
        o_ref[...] = out


def _attn_sample(q, k_new, v_new, bias_rows, cache_k, cache_v, page_table, layer, n_heads):
    n_b, rows, width = q.shape
    n_q = rows // n_heads
    n_pages = page_table.shape[1]
    page = SB_BLOCK
    pps = SAMPLE_PAGES_PER_STEP
    assert n_pages % pps == 0
    n_steps = n_pages // pps

    def page_spec(p):
        return pl.BlockSpec((None, None, page * n_heads, HEAD_DIM),
                            lambda b, j, pt: (pt[b, n_pages - 1 - (j * pps + p)], layer, 0, 0))

    grid_spec = pltpu.PrefetchScalarGridSpec(
        num_scalar_prefetch=1,
        grid=(n_b, n_steps),
        in_specs=[pl.BlockSpec((None, rows, width), lambda b, j, pt: (b, 0, 0)),
                  pl.BlockSpec((None, page, width), lambda b, j, pt: (b, 0, 0)),
                  pl.BlockSpec((None, page, width), lambda b, j, pt: (b, 0, 0)),
                  pl.BlockSpec((rows, 1), lambda b, j, pt: (0, 0))]
                 + [page_spec(p) for p in range(pps)] * 1
                 + [page_spec(p) for p in range(pps)],
        out_specs=pl.BlockSpec((None, n_q, width), lambda b, j, pt: (b, 0, 0)),
        scratch_shapes=[pltpu.VMEM((rows, width), BF16), pltpu.VMEM((rows, width), F32),
                        pltpu.VMEM((rows, page), F32)],
    )
    return pl.pallas_call(
        functools.partial(_attn_sample_kernel, n_heads=n_heads, n_q=n_q, pages_per_step=pps),
        grid_spec=grid_spec,
        out_shape=jax.ShapeDtypeStruct((n_b, n_q, width), F32),
        compiler_params=_cparams(("arbitrary", "arbitrary"), 48),
        name="attn_sample",
    )(page_table, q, k_new, v_new, bias_rows, *([cache_k] * pps), *([cache_v] * pps))


def _spatial_gate(u_act, v_norm, w_of_head, b_of_head, keep, n_heads):
    outs = []
    for h in range(n_heads):
        w = jnp.where(keep, w_of_head(h), 0.0).astype(BF16)
        v_h = v_norm[:, h * HEAD_DIM:(h + 1) * HEAD_DIM].astype(BF16)
        outs.append(jnp.dot(w, v_h, preferred_element_type=F32) + b_of_head(h))
    return u_act * jnp.concatenate(outs, axis=1)


def _group_a_prompt_kernel(u_ref, v_ref, ws_ref, bs_ref, lg_ref, lb_ref, og_ref, o_ref, *, n_heads):
    u_act = _gelu(u_ref[...])
    v_norm = _layer_norm(_gelu(v_ref[...]), lg_ref[...], lb_ref[...])
    r = lax.broadcasted_iota(I32, (CHUNK, CHUNK), 0)
    c = lax.broadcasted_iota(I32, (CHUNK, CHUNK), 1)
    y = _spatial_gate(u_act, v_norm, lambda h: ws_ref[h], lambda h: bs_ref[h], c <= r, n_heads)
    o_ref[...] = _rms_norm(y, og_ref[...]).astype(BF16)


def _group_a_prompt(proj, w_s, b_s, lnv_g, lnv_b, og, n_rows, d_a):
    n_heads = d_a // HEAD_DIM
    vec = pl.BlockSpec((1, d_a), lambda c: (0, 0))
    return pl.pallas_call(
        functools.partial(_group_a_prompt_kernel, n_heads=n_heads),
        grid=(n_rows // CHUNK,),
        in_specs=[pl.BlockSpec((CHUNK, d_a), lambda c: (c, 0)),
                  pl.BlockSpec((CHUNK, d_a), lambda c: (c, 1)),
                  pl.BlockSpec((n_heads, CHUNK, CHUNK), lambda c: (0, 0, 0)),
                  pl.BlockSpec((n_heads, CHUNK, 1), lambda c: (0, 0, 0)),
                  vec, vec, vec],
        out_specs=pl.BlockSpec((CHUNK, d_a), lambda c: (c, 0)),
        out_shape=jax.ShapeDtypeStruct((n_rows, d_a), BF16),
        compiler_params=_cparams(("arbitrary",), 16),
        name="group_a_prompt",
    )(proj, proj, w_s, b_s.reshape(n_heads, CHUNK, 1), lnv_g.reshape(1, d_a), lnv_b.reshape(1, d_a),
      og.reshape(1, d_a))


CONV_HALO = 8


def _group_c_prompt_kernel(gb_ref, gc_ref, xc_ref, gch_ref, xch_ref, cw_ref, og_ref, o_ref, new_ref,
                           *, tc):
    t = pl.program_id(1)
    p = gc_ref[...] * xc_ref[...]
    halo = jnp.where(t > 0, gch_ref[...] * xch_ref[...], 0.0)
    ext = jnp.concatenate([halo, p], axis=0)
    p1 = pltpu.roll(ext, 1, 0)[CONV_HALO:, :]
    p2 = pltpu.roll(ext, 2, 0)[CONV_HALO:, :]
    cw = cw_ref[...]
    conv = cw[2:3, :] * p
    conv = conv + cw[0:1, :] * p2
    conv = conv + cw[1:2, :] * p1
    o_ref[...] = _rms_norm(gb_ref[...] * conv, og_ref[...]).astype(BF16)

    @pl.when(t == pl.num_programs(1) - 1)
    def _():
        new_ref[...] = p[tc - 2:tc, :]


def _group_c_prompt(proj, conv_w, og, n_batch, seq, d_c, col_gb):
    cb = col_gb // d_c
    tc = min(512, seq)
    n_t = seq // tc
    tile = lambda col: pl.BlockSpec((tc, d_c), lambda b, t: (b * n_t + t, col))
    halo = lambda col: pl.BlockSpec(
        (CONV_HALO, d_c), lambda b, t: (jnp.maximum((b * seq + t * tc) // CONV_HALO - 1, 0), col))
    return pl.pallas_call(
        functools.partial(_group_c_prompt_kernel, tc=tc),
        grid=(n_batch, n_t),
        in_specs=[tile(cb), tile(cb + 1), tile(cb + 2), halo(cb + 1), halo(cb + 2),
                  pl.BlockSpec((3, d_c), lambda b, t: (0, 0)), pl.BlockSpec((1, d_c), lambda b, t: (0, 0))],
        out_specs=[pl.BlockSpec((tc, d_c), lambda b, t: (b * n_t + t, 0)),
                   pl.BlockSpec((None, 2, d_c), lambda b, t: (b, 0, 0))],
        out_shape=[jax.ShapeDtypeStruct((n_batch * seq, d_c), BF16),
                   jax.ShapeDtypeStruct((n_batch, 2, d_c), F32)],
        compiler_params=_cparams(("arbitrary", "arbitrary"), 32),
        name="group_c_prompt",
    )(proj, proj, proj, proj, proj, conv_w, og.reshape(1, d_c))


def _mixer_cat_prompt_kernel(ya_ref, yb_ref, yc_ref, og_ref, o_ref, *, d_a, d_b):
    o_ref[:, 0:d_a] = ya_ref[...]
    o_ref[:, d_a:d_a + d_b] = _rms_norm(yb_ref[...], og_ref[...]).astype(BF16)
    o_ref[:, d_a + d_b:] = yc_ref[...]


def _mixer_cat_prompt(ya, yb, yc, og_b, tm):
    n_rows, d_a = ya.shape
    d_b = yb.shape[1]
    d_c = yc.shape[1]
    return pl.pallas_call(
        functools.partial(_mixer_cat_prompt_kernel, d_a=d_a, d_b=d_b),
        grid=(n_rows // tm,),
        in_specs=[pl.BlockSpec((tm, d_a), lambda i: (i, 0)), pl.BlockSpec((tm, d_b), lambda i: (i, 0)),
                  pl.BlockSpec((tm, d_c), lambda i: (i, 0)), pl.BlockSpec((1, d_b), lambda i: (0, 0))],
        out_specs=pl.BlockSpec((tm, d_a + d_b + d_c), lambda i: (i, 0)),
        out_shape=jax.ShapeDtypeStruct((n_rows, d_a + d_b + d_c), BF16),
        compiler_params=_cparams(("arbitrary",), 32),
        name="mixer_cat_prompt",
    )(ya, yb, yc, og_b.reshape(1, d_b))


def _mixer_sample_kernel(proj_ref, yb_ref, wt_ref, bt_ref, lg_ref, lb_ref, cw_ref, e1_ref, e2_ref,
                         og_ref, o_ref, va_ref, p_ref, *, n_q, d_a, d_b, d_c):
    n_rows = proj_ref.shape[0]
    n_heads_a = d_a // HEAD_DIM
    c_gb = 2 * d_a + 3 * d_b
    og = og_ref[...]
    u_act = _gelu(proj_ref[:, 0:d_a])
    v_norm = _layer_norm(_gelu(proj_ref[:, d_a:2 * d_a]), lg_ref[...], lb_ref[...])
    va_ref[...] = v_norm
    r = lax.broadcasted_iota(I32, (n_rows, n_rows), 0)
    c = lax.broadcasted_iota(I32, (n_rows, n_rows), 1)
    keep = jnp.logical_and(c <= r, r // n_q == c // n_q)
    ya = _spatial_gate(u_act, v_norm, lambda h: wt_ref[h], lambda h: bt_ref[h], keep, n_heads_a)
    o_ref[:, 0:d_a] = _rms_norm(ya, og[:, 0:d_a]).astype(BF16)
    o_ref[:, d_a:d_a + d_b] = _rms_norm(yb_ref[...], og[:, d_a:d_a + d_b]).astype(BF16)
    p = proj_ref[:, c_gb + d_c:c_gb + 2 * d_c] * proj_ref[:, c_gb + 2 * d_c:c_gb + 3 * d_c]
    p_ref[...] = p
    pos = lax.broadcasted_iota(I32, p.shape, 0) % n_q
    p1 = jnp.where(pos >= 1, pltpu.roll(p, 1, 0), e1_ref[...])
    p2 = jnp.where(pos >= 2, pltpu.roll(p, 2, 0), e2_ref[...])
    cw = cw_ref[...]
    conv = cw[2:3, :] * p
    conv = conv + cw[0:1, :] * p2
    conv = conv + cw[1:2, :] * p1
    yc = proj_ref[:, c_gb:c_gb + d_c] * conv
    o_ref[:, d_a + d_b:] = _rms_norm(yc, og[:, d_a + d_b:]).astype(BF16)


def _mixer_sample(proj, row0, yb, w_s, b_s, lnv_g, lnv_b, conv_w, state, og, n_b, n_q, d_a, d_b, d_c):
    n = n_b * n_q
    assert row0 % n == 0
    d_in = proj.shape[1]
    n_heads_a = d_a // HEAD_DIM
    d_mix = d_a + d_b + d_c
    wt = jnp.tile(w_s[:, :n_q, :n_q], (1, n_b, n_b))
    bt = jnp.tile(b_s[:, :n_q], (1, n_b)).reshape(n_heads_a, n, 1)
    e1 = jnp.repeat(state[:, 1:2, :], n_q, axis=1).reshape(n, d_c)
    e2 = jnp.tile(state, (1, n_q // 2, 1)).reshape(n, d_c)
    full = lambda shape: pl.BlockSpec(shape, lambda i: (0,) * len(shape))
    return pl.pallas_call(
        functools.partial(_mixer_sample_kernel, n_q=n_q, d_a=d_a, d_b=d_b, d_c=d_c),
        grid=(1,),
        in_specs=[pl.BlockSpec((n, d_in), lambda i: (row0 // n, 0)), full((n, d_b)),
                  full((n_heads_a, n, n)), full((n_heads_a, n, 1)), full((1, d_a)), full((1, d_a)),
                  full((3, d_c)), full((n, d_c)), full((n, d_c)), full((1, d_mix))],
        out_specs=[full((n, d_mix)), full((n, d_a)), full((n, d_c))],
        out_shape=[jax.ShapeDtypeStruct((n, d_mix), BF16), jax.ShapeDtypeStruct((n, d_a), F32),
                   jax.ShapeDtypeStruct((n, d_c), F32)],
        compiler_params=_cparams(("arbitrary",), 16),
        name="mixer_sample",
    )(proj, yb, wt, bt, lnv_g.reshape(1, d_a), lnv_b.reshape(1, d_a), conv_w, e1, e2,
      og.reshape(1, d_mix))


def kernel(x_prompt, x_sample, cache_k, cache_v, state_conv, page_table, p_prompt, p_sample, w_in, w_spatial, b_spatial, lnv_g, lnv_b, conv_w, sb_bias, out_norm_g, w_out, ln1_g, ln1_b, ln2_g, ln2_b, w_ffn_gate, w_ffn_up, w_ffn_down, w_router, w_exp_gate, w_exp_up, w_exp_down, w_ple_gate, w_ple_proj):
    n_b, seq, d = x_prompt.shape
    n_db, n_q, _ = x_sample.shape
    depth = w_in.shape[0]
    n_heads_b = cache_k.shape[3]
    page = cache_k.shape[2]
    assert page == SB_BLOCK and cache_k.shape[4] == HEAD_DIM and seq % SB_BLOCK == 0
    d_a = lnv_g.shape[1]
    d_b = n_heads_b * HEAD_DIM
    d_c = conv_w.shape[2]
    n_exp = w_router.shape[-1]
    n_p = n_b * seq
    n_s = n_db * n_q
    alpha = float((2 * depth) ** 0.25)
    col_q, col_k, col_v = 2 * d_a, 2 * d_a + d_b, 2 * d_a + 2 * d_b
    col_gb = 2 * d_a + 3 * d_b

    xf = jnp.concatenate([x_prompt.reshape(n_p, d), x_sample.reshape(n_s, d)], axis=0)
    xb = xf.astype(BF16)
    ck = cache_k.reshape(cache_k.shape[0], depth, page * n_heads_b, HEAD_DIM)
    cv = cache_v.reshape(cache_v.shape[0], depth, page * n_heads_b, HEAD_DIM)
    page_table = page_table.astype(I32)

    kp, vp, convp, ks, vs, convs, vas = [], [], [], [], [], [], []
    for i in range(depth):
        og = out_norm_g[i]
        proj = _mm(xb, w_in, i, F32, 512, 1408, "proj_in")
        ya = _group_a_prompt(proj, w_spatial[i], b_spatial[i], lnv_g[i], lnv_b[i], og[:d_a], n_p, d_a)
        yb = _attn_prompt(proj, sb_bias[i], n_b, seq, n_heads_b, col_q, col_k, col_v)
        yc, conv_new_p = _group_c_prompt(proj, conv_w[i], og[d_a + d_b:], n_b, seq, d_c, col_gb)
        ycat_p = _mixer_cat_prompt(ya, yb, yc, og[d_a:d_a + d_b], 512)
        proj_s = proj[n_p:]
        q_s = proj_s[:, col_q:col_k].reshape(n_db, n_q, d_b)
        k_s = proj_s[:, col_k:col_v].reshape(n_db, n_q, d_b)
        v_s = proj_s[:, col_v:col_gb].reshape(n_db, n_q, d_b)
        pad = ((0, 0), (0, page - n_q), (0, 0))
        bias_rows = jnp.repeat(sb_bias[i], n_q).reshape(n_heads_b * n_q, 1)
        yb_s = _attn_sample(jnp.tile(q_s, (1, n_heads_b, 1)), jnp.pad(k_s, pad), jnp.pad(v_s, pad),
                            bias_rows, ck, cv, page_table, i, n_heads_b)
        ycat_s, va_s, pc_s = _mixer_sample(proj, n_p, yb_s.reshape(n_s, d_b), w_spatial[i], b_spatial[i],
                                           lnv_g[i], lnv_b[i], conv_w[i], state_conv[:, i], og,
                                           n_db, n_q, d_a, d_b, d_c)
        ycat = jnp.concatenate([ycat_p, ycat_s], axis=0)
        kp.append(proj[:n_p, col_k:col_v].reshape(n_b, seq, n_heads_b, HEAD_DIM))
        vp.append(proj[:n_p, col_v:col_gb].reshape(n_b, seq, n_heads_b, HEAD_DIM))
        convp.append(conv_new_p)
        ks.append(k_s.reshape(n_db, n_q, n_heads_b, HEAD_DIM))
        vs.append(v_s.reshape(n_db, n_q, n_heads_b, HEAD_DIM))
        convs.append(pc_s.reshape(n_db, n_q, d_c)[:, n_q - 2:])
        vas.append(va_s.reshape(n_db, n_q, d_a))
        h = _mm(ycat, w_out, i, F32, 1024, 1024, "proj_out")
        jl = i // 2
        if i % 2 == 0:
            x1f, x1b = _ln_res(xf, h, ln1_g[i], ln1_b[i], alpha, 256)
            hid = _swiglu_up(x1b, w_ffn_gate, w_ffn_up, jl, 1024, 512)
            f = _mm(hid, w_ffn_down, jl, F32, 512, 512, "ffn_down")
            x2f, x2b = _ln_res(x1f, f, ln2_g[i], ln2_b[i], alpha, 256)
        else:
            x1f, x1b, idx, gates = _ln_res(xf, h, ln1_g[i], ln1_b[i], alpha, 256, w_router=w_router[jl])
            dest, row_token, tile_expert, n_used = _route(idx, n_exp, MOE_TILE)
            xs = jnp.take(x1b, row_token, axis=0, mode="clip")
            hid = _moe_up(xs, w_exp_gate, w_exp_up, jl, tile_expert, n_used, 1024)
            ys = _moe_down(hid, w_exp_down, jl, tile_expert, n_used, 512)
            x2f, x2b = _ln_res_combine(x1f, ys, dest, gates, ln2_g[i], ln2_b[i], alpha, 128)
        p_i = jnp.concatenate([p_prompt[i].reshape(n_p, -1), p_sample[i].reshape(n_s, -1)], axis=0)
        xf, xb = _ple(x2b, x2f, p_i, w_ple_gate, w_ple_proj, i, 512, 1024)

    return (xf[:n_p].reshape(n_b, seq, d), xf[n_p:].reshape(n_db, n_q, d),
            jnp.stack(kp, axis=1), jnp.stack(vp, axis=1), jnp.stack(convp, axis=1),
            jnp.stack(ks, axis=1), jnp.stack(vs, axis=1), jnp.stack(convs, axis=1),
            jnp.stack(vas, axis=1))
```

```python
import functools

import jax
import jax.numpy as jnp
from jax import lax
from jax.experimental import pallas as pl
from jax.experimental.pallas import tpu as pltpu

F32 = jnp.float32
BF16 = jnp.bfloat16
I32 = jnp.int32

HEAD_DIM = 128
CHUNK = 128
TOP_K = 2
LN_EPS = 1e-5
RMS_EPS = 1e-6
SB_BLOCK = 128
SB_PROMPT_TILE = 512
SB_SCALE = HEAD_DIM ** -0.5
LANE = 128
MIB = 1024 * 1024
MOE_TILE = 256
SAMPLE_PAGES_PER_STEP = 8
DMA_LOOP_UNROLL = 8


def _cparams(semantics, vmem_mib):
    return pltpu.CompilerParams(dimension_semantics=semantics, vmem_limit_bytes=vmem_mib * MIB)


def _pick_tile(n, pref, align=LANE):
    if n <= pref:
        return n
    best = None
    for t in range(align, pref + 1, align):
        if n % t == 0:
            best = t
    assert best is not None, (n, pref)
    return best


def _row_tile_dispatch(i, n_tiles, tm, m_total, fn):
    rem = m_total - (n_tiles - 1) * tm
    if rem == tm:
        fn(tm)
    else:
        pl.when(i < n_tiles - 1)(lambda: fn(tm))
        pl.when(i == n_tiles - 1)(lambda: fn(rem))


def _layer_norm(xf, g, b):
    mu = jnp.mean(xf, axis=-1, keepdims=True)
    xc = xf - mu
    var = jnp.mean(xc * xc, axis=-1, keepdims=True)
    return xc * lax.rsqrt(var + LN_EPS) * g + b


def _rms_norm(xf, g):
    ms = jnp.mean(xf * xf, axis=-1, keepdims=True)
    return xf * lax.rsqrt(ms + RMS_EPS) * g


def _gelu(x):
    return x * (lax.erf(x / (2.0 ** 0.5)) + 1.0) / 2.0


def _mm_kernel(x_ref, w_ref, o_ref, wb_ref, *, tm, m_total, n_tiles):
    i = pl.program_id(1)

    @pl.when(i == 0)
    def _():
        wb_ref[...] = w_ref[...].astype(BF16)

    def fn(rows):
        acc = jnp.dot(x_ref[0:rows, :], wb_ref[...], preferred_element_type=F32)
        o_ref[0:rows, :] = acc.astype(o_ref.dtype)

    _row_tile_dispatch(i, n_tiles, tm, m_total, fn)


def _mm(x, w_stack, layer, out_dtype, tm, tn, name):
    m, k = x.shape
    n = w_stack.shape[-1]
    tm = min(tm, m)
    tn = _pick_tile(n, tn)
    n_tiles = pl.cdiv(m, tm)
    vmem = (2 * k * tn * 4 + k * tn * 2 + 2 * tm * k * 2 + 3 * tm * tn * 4) // MIB + 6
    return pl.pallas_call(
        functools.partial(_mm_kernel, tm=tm, m_total=m, n_tiles=n_tiles),
        grid=(n // tn, n_tiles),
        in_specs=[pl.BlockSpec((tm, k), lambda j, i: (i, 0)),
                  pl.BlockSpec((None, k, tn), lambda j, i: (layer, 0, j))],
        out_specs=pl.BlockSpec((tm, tn), lambda j, i: (i, j)),
        out_shape=jax.ShapeDtypeStruct((m, n), out_dtype),
        scratch_shapes=[pltpu.VMEM((k, tn), BF16)],
        compiler_params=_cparams(("arbitrary", "arbitrary"), vmem),
        name=name,
    )(x, w_stack)


def _swiglu_up_kernel(x_ref, wg_ref, wu_ref, o_ref, wgb_ref, wub_ref, *, tm, m_total, n_tiles):
    i = pl.program_id(1)

    @pl.when(i == 0)
    def _():
        wgb_ref[...] = wg_ref[...].astype(BF16)
        wub_ref[...] = wu_ref[...].astype(BF16)

    def fn(rows):
        x = x_ref[0:rows, :]
        g = jnp.dot(x, wgb_ref[...], preferred_element_type=F32)
        u = jnp.dot(x, wub_ref[...], preferred_element_type=F32)
        o_ref[0:rows, :] = (g * jax.nn.sigmoid(g) * u).astype(o_ref.dtype)

    _row_tile_dispatch(i, n_tiles, tm, m_total, fn)


def _swiglu_up(x, wg_stack, wu_stack, layer, tm, tn):
    m, k = x.shape
    n = wg_stack.shape[-1]
    tm = min(tm, m)
    tn = _pick_tile(n, tn)
    n_tiles = pl.cdiv(m, tm)
    vmem = (4 * k * tn * 4 + 2 * k * tn * 2 + 2 * tm * k * 2 + 5 * tm * tn * 4) // MIB + 6
    w_spec = pl.BlockSpec((None, k, tn), lambda j, i: (layer, 0, j))
    return pl.pallas_call(
        functools.partial(_swiglu_up_kernel, tm=tm, m_total=m, n_tiles=n_tiles),
        grid=(n // tn, n_tiles),
        in_specs=[pl.BlockSpec((tm, k), lambda j, i: (i, 0)), w_spec, w_spec],
        out_specs=pl.BlockSpec((tm, tn), lambda j, i: (i, j)),
        out_shape=jax.ShapeDtypeStruct((m, n), BF16),
        scratch_shapes=[pltpu.VMEM((k, tn), BF16), pltpu.VMEM((k, tn), BF16)],
        compiler_params=_cparams(("arbitrary", "arbitrary"), vmem),
        name="ffn_up",
    )(x, wg_stack, wu_stack)


def _ple_kernel(xb_ref, xf_ref, p_ref, wg_ref, wp_ref, of_ref, ob_ref, wgb_ref, wpb_ref,
                *, tm, m_total, n_tiles):
    i = pl.program_id(1)

    @pl.when(i == 0)
    def _():
        wgb_ref[...] = wg_ref[...].astype(BF16)
        wpb_ref[...] = wp_ref[...].astype(BF16)

    def fn(rows):
        gate = jnp.dot(xb_ref[0:rows, :], wgb_ref[...], preferred_element_type=F32)
        proj = jnp.dot(p_ref[0:rows, :].astype(BF16), wpb_ref[...], preferred_element_type=F32)
        y = xf_ref[0:rows, :] + jax.nn.sigmoid(gate) * proj
        of_ref[0:rows, :] = y
        ob_ref[0:rows, :] = y.astype(BF16)

    _row_tile_dispatch(i, n_tiles, tm, m_total, fn)


def _ple(xb, xf, p, wg_stack, wp_stack, layer, tm, tn):
    m, d = xf.shape
    pd = p.shape[1]
    tm = min(tm, m)
    tn = _pick_tile(d, tn)
    n_tiles = pl.cdiv(m, tm)
    vmem = (2 * d * tn * 4 + d * tn * 2 + 3 * pd * tn * 4 + 2 * tm * d * 2 + 2 * tm * pd * 4
            + 9 * tm * tn * 4) // MIB + 6
    return pl.pallas_call(
        functools.partial(_ple_kernel, tm=tm, m_total=m, n_tiles=n_tiles),
        grid=(d // tn, n_tiles),
        in_specs=[pl.BlockSpec((tm, d), lambda j, i: (i, 0)),
                  pl.BlockSpec((tm, tn), lambda j, i: (i, j)),
                  pl.BlockSpec((tm, pd), lambda j, i: (i, 0)),
                  pl.BlockSpec((None, d, tn), lambda j, i: (layer, 0, j)),
                  pl.BlockSpec((None, pd, tn), lambda j, i: (layer, 0, j))],
        out_specs=[pl.BlockSpec((tm, tn), lambda j, i: (i, j)),
                   pl.BlockSpec((tm, tn), lambda j, i: (i, j))],
        out_shape=[jax.ShapeDtypeStruct((m, d), F32), jax.ShapeDtypeStruct((m, d), BF16)],
        scratch_shapes=[pltpu.VMEM((d, tn), BF16), pltpu.VMEM((pd, tn), BF16)],
        compiler_params=_cparams(("arbitrary", "arbitrary"), vmem),
        name="ple",
    )(xb, xf, p, wg_stack, wp_stack)


def _top2(logits):
    n_e = logits.shape[-1]
    lane = lax.broadcasted_iota(I32, logits.shape, 1)
    m1 = jnp.max(logits, axis=-1, keepdims=True)
    i1 = jnp.min(jnp.where(logits == m1, lane, n_e), axis=-1, keepdims=True)
    rest = jnp.where(lane == i1, -jnp.inf, logits)
    m2 = jnp.max(rest, axis=-1, keepdims=True)
    i2 = jnp.min(jnp.where(rest == m2, lane, n_e), axis=-1, keepdims=True)
    e2 = jnp.exp(m2 - m1)
    g1 = 1.0 / (1.0 + e2)
    g2 = e2 / (1.0 + e2)
    slot = lax.broadcasted_iota(I32, (logits.shape[0], TOP_K), 1)
    return jnp.where(slot == 0, i1, i2), jnp.where(slot == 0, g1, g2)


def _ln_res_kernel(x_ref, h_ref, g_ref, b_ref, of_ref, ob_ref, *, alpha):
    y = _layer_norm(alpha * x_ref[...] + h_ref[...], g_ref[...], b_ref[...])
    of_ref[...] = y
    ob_ref[...] = y.astype(BF16)


def _ln_res_router_kernel(x_ref, h_ref, g_ref, b_ref, wr_ref, of_ref, ob_ref, idx_ref, gate_ref,
                          *, alpha):
    y = _layer_norm(alpha * x_ref[...] + h_ref[...], g_ref[...], b_ref[...])
    of_ref[...] = y
    ob_ref[...] = y.astype(BF16)
    logits = jnp.dot(y.astype(BF16), wr_ref[...].astype(BF16), preferred_element_type=F32)
    idx, gates = _top2(logits)
    idx_ref[...] = idx
    gate_ref[...] = gates


def _ln_res(x, h, g, b, alpha, tm, w_router=None):
    m, d = x.shape
    tm = min(tm, m)
    row = pl.BlockSpec((tm, d), lambda i: (i, 0))
    vec = pl.BlockSpec((1, d), lambda i: (0, 0))
    out_specs = [row, row]
    out_shape = [jax.ShapeDtypeStruct((m, d), F32), jax.ShapeDtypeStruct((m, d), BF16)]
    in_specs = [row, row, vec, vec]
    args = [x, h, g.reshape(1, d), b.reshape(1, d)]
    if w_router is None:
        body = functools.partial(_ln_res_kernel, alpha=alpha)
    else:
        n_e = w_router.shape[-1]
        body = functools.partial(_ln_res_router_kernel, alpha=alpha)
        in_specs.append(pl.BlockSpec((d, n_e), lambda i: (0, 0)))
        args.append(w_router)
        pair = pl.BlockSpec((tm, TOP_K), lambda i: (i, 0))
        out_specs += [pair, pair]
        out_shape += [jax.ShapeDtypeStruct((m, TOP_K), I32), jax.ShapeDtypeStruct((m, TOP_K), F32)]
    return pl.pallas_call(
        body, grid=(pl.cdiv(m, tm),), in_specs=in_specs, out_specs=out_specs, out_shape=out_shape,
        compiler_params=_cparams(("arbitrary",), 12 * tm * d * 4 // MIB + 8),
        name="ln_res" if w_router is None else "ln_res_router",
    )(*args)


def _ln_res_combine_kernel(dest_ref, x_ref, gate_ref, g_ref, b_ref, ys_ref, of_ref, ob_ref, ybuf_ref,
                           sem_ref, *, alpha, tm, n_tok):
    base = pl.program_id(0) * tm

    def row_copy(r, slot):
        tok = jnp.minimum(base + r, n_tok - 1)
        src = dest_ref[tok * TOP_K + slot]
        return pltpu.make_async_copy(ys_ref.at[pl.ds(src, 1), :], ybuf_ref.at[slot, pl.ds(r, 1), :],
                                     sem_ref.at[slot])

    def start(r, carry):
        for slot in range(TOP_K):
            row_copy(r, slot).start()
        return carry

    def wait(r, carry):
        for slot in range(TOP_K):
            row_copy(r, slot).wait()
        return carry

    lax.fori_loop(0, tm, start, 0, unroll=DMA_LOOP_UNROLL)
    lax.fori_loop(0, tm, wait, 0, unroll=DMA_LOOP_UNROLL)
    gates = gate_ref[...]
    f = gates[:, 0:1] * ybuf_ref[0] + gates[:, 1:2] * ybuf_ref[1]
    y = _layer_norm(alpha * x_ref[...] + f, g_ref[...], b_ref[...])
    of_ref[...] = y
    ob_ref[...] = y.astype(BF16)


def _ln_res_combine(x, ys, dest, gates, g, b, alpha, tm):
    m, d = x.shape
    tm = min(tm, m)
    row = pl.BlockSpec((tm, d), lambda i, dst: (i, 0))
    vec = pl.BlockSpec((1, d), lambda i, dst: (0, 0))
    grid_spec = pltpu.PrefetchScalarGridSpec(
        num_scalar_prefetch=1,
        grid=(pl.cdiv(m, tm),),
        in_specs=[row, pl.BlockSpec((tm, TOP_K), lambda i, dst: (i, 0)), vec, vec,
                  pl.BlockSpec(memory_space=pl.ANY)],
        out_specs=[row, pl.BlockSpec((tm, d), lambda i, dst: (i, 0))],
        scratch_shapes=[pltpu.VMEM((TOP_K, tm, d), F32), pltpu.SemaphoreType.DMA((TOP_K,))],
    )
    return pl.pallas_call(
        functools.partial(_ln_res_combine_kernel, alpha=alpha, tm=tm, n_tok=m),
        grid_spec=grid_spec,
        out_shape=[jax.ShapeDtypeStruct((m, d), F32), jax.ShapeDtypeStruct((m, d), BF16)],
        compiler_params=_cparams(("arbitrary",), 14 * tm * d * 4 // MIB + 8),
        name="ln_res_combine",
    )(dest.reshape(-1), x, gates, g.reshape(1, d), b.reshape(1, d), ys)


def _is_new_expert(te_ref, t):
    prev = te_ref[jnp.maximum(t - 1, 0)]
    return jnp.logical_or(t == 0, te_ref[t] != prev)


def _moe_dispatch_kernel(tok_ref, nu_ref, x_ref, o_ref, xbuf_ref, sem_ref, *, tm):
    t = pl.program_id(0)
    base = t * tm

    def row_copy(r):
        return pltpu.make_async_copy(x_ref.at[pl.ds(tok_ref[base + r], 1), :], xbuf_ref.at[pl.ds(r, 1), :],
                                     sem_ref.at[0])

    def start(r, carry):
        row_copy(r).start()
        return carry

    def wait(r, carry):
        row_copy(r).wait()
        return carry

    @pl.when(t < nu_ref[0])
    def _():
        lax.fori_loop(0, tm, start, 0, unroll=DMA_LOOP_UNROLL)
        lax.fori_loop(0, tm, wait, 0, unroll=DMA_LOOP_UNROLL)
        o_ref[...] = xbuf_ref[...].astype(BF16)


def _moe_dispatch(xf, row_token, n_used):
    d = xf.shape[1]
    r = row_token.shape[0]
    tm = MOE_TILE
    grid_spec = pltpu.PrefetchScalarGridSpec(
        num_scalar_prefetch=2,
        grid=(r // tm,),
        in_specs=[pl.BlockSpec(memory_space=pl.ANY)],
        out_specs=pl.BlockSpec((tm, d), lambda t, tok, nu: (jnp.minimum(t, nu[0] - 1), 0)),
        scratch_shapes=[pltpu.VMEM((tm, d), F32), pltpu.SemaphoreType.DMA((1,))],
    )
    return pl.pallas_call(
        functools.partial(_moe_dispatch_kernel, tm=tm),
        grid_spec=grid_spec,
        out_shape=jax.ShapeDtypeStruct((r, d), BF16),
        compiler_params=_cparams(("arbitrary",), 8 * tm * d * 4 // MIB + 8),
        name="moe_dispatch",
    )(row_token, n_used, xf)


def _moe_up_kernel(te_ref, nu_ref, x_ref, wg_ref, wu_ref, o_ref, wgb_ref, wub_ref):
    t = pl.program_id(1)

    @pl.when(_is_new_expert(te_ref, t))
    def _():
        wgb_ref[...] = wg_ref[...].astype(BF16)
        wub_ref[...] = wu_ref[...].astype(BF16)

    @pl.when(t < nu_ref[0])
    def _():
        x = x_ref[...]
        g = jnp.dot(x, wgb_ref[...], preferred_element_type=F32)
        u = jnp.dot(x, wub_ref[...], preferred_element_type=F32)
        o_ref[...] = (g * jax.nn.sigmoid(g) * u).astype(o_ref.dtype)


def _moe_down_kernel(te_ref, nu_ref, x_ref, w_ref, o_ref, wb_ref):
    t = pl.program_id(1)

    @pl.when(_is_new_expert(te_ref, t))
    def _():
        wb_ref[...] = w_ref[...].astype(BF16)

    @pl.when(t < nu_ref[0])
    def _():
        o_ref[...] = jnp.dot(x_ref[...], wb_ref[...], preferred_element_type=F32)


def _moe_up(xs, wg_stack, wu_stack, jl, tile_expert, n_used, tn):
    r, d = xs.shape
    n = wg_stack.shape[-1]
    tn = _pick_tile(n, tn)
    tm = MOE_TILE
    w_spec = pl.BlockSpec((None, None, d, tn), lambda j, t, te, nu: (jl, te[t], 0, j))
    vmem = (4 * d * tn * 4 + 2 * d * tn * 2 + 2 * tm * d * 2 + 5 * tm * tn * 4) // MIB + 6
    grid_spec = pltpu.PrefetchScalarGridSpec(
        num_scalar_prefetch=2,
        grid=(n // tn, r // tm),
        in_specs=[pl.BlockSpec((tm, d), lambda j, t, te, nu: (jnp.minimum(t, nu[0] - 1), 0)),
                  w_spec, w_spec],
        out_specs=pl.BlockSpec((tm, tn), lambda j, t, te, nu: (jnp.minimum(t, nu[0] - 1), j)),
        scratch_shapes=[pltpu.VMEM((d, tn), BF16), pltpu.VMEM((d, tn), BF16)],
    )
    return pl.pallas_call(
        _moe_up_kernel, grid_spec=grid_spec,
        out_shape=jax.ShapeDtypeStruct((r, n), BF16),
        compiler_params=_cparams(("arbitrary", "arbitrary"), vmem),
        name="moe_up",
    )(tile_expert, n_used, xs, wg_stack, wu_stack)


def _moe_down(hs, wd_stack, jl, tile_expert, n_used, tn):
    r, k = hs.shape
    n = wd_stack.shape[-1]
    tn = _pick_tile(n, tn)
    tm = MOE_TILE
    vmem = (2 * k * tn * 4 + k * tn * 2 + 2 * tm * k * 2 + 3 * tm * tn * 4) // MIB + 6
    grid_spec = pltpu.PrefetchScalarGridSpec(
        num_scalar_prefetch=2,
        grid=(n // tn, r // tm),
        in_specs=[pl.BlockSpec((tm, k), lambda j, t, te, nu: (jnp.minimum(t, nu[0] - 1), 0)),
                  pl.BlockSpec((None, None, k, tn), lambda j, t, te, nu: (jl, te[t], 0, j))],
        out_specs=pl.BlockSpec((tm, tn), lambda j, t, te, nu: (jnp.minimum(t, nu[0] - 1), j)),
        scratch_shapes=[pltpu.VMEM((k, tn), BF16)],
    )
    return pl.pallas_call(
        _moe_down_kernel, grid_spec=grid_spec,
        out_shape=jax.ShapeDtypeStruct((r, n), F32),
        compiler_params=_cparams(("arbitrary", "arbitrary"), vmem),
        name="moe_down",
    )(tile_expert, n_used, hs, wd_stack)


def _route(idx, n_exp, tile):
    n_tok = idx.shape[0]
    n_asg = n_tok * TOP_K
    nt_max = (n_asg + n_exp * (tile - 1)) // tile
    flat = idx.reshape(-1)
    onehot = (flat[:, None] == jnp.arange(n_exp, dtype=I32)[None, :]).astype(I32)
    csum = jnp.cumsum(onehot, axis=0)
    rank = jnp.take_along_axis(csum, flat[:, None], axis=1)[:, 0] - 1
    counts = csum[-1]
    ntile_e = (counts + tile - 1) // tile
    tile_end = jnp.cumsum(ntile_e)
    tile_start = tile_end - ntile_e
    dest = tile_start[flat] * tile + rank
    n_used = tile_end[-1]
    tile_ids = jnp.arange(nt_max, dtype=I32)
    te = jnp.sum((tile_ids[:, None] >= tile_end[None, :]).astype(I32), axis=1)
    te = jnp.minimum(te, n_exp - 1)
    te = jnp.where(tile_ids < n_used, te, te[n_used - 1])
    row_token = jnp.zeros((nt_max * tile,), I32).at[dest].set(jnp.arange(n_asg, dtype=I32) // TOP_K)
    return dest.reshape(n_tok, TOP_K), row_token, te.astype(I32), n_used.reshape(1).astype(I32)


def _sb_tile(q, k, v, bias, carry, acc, suffix_ext, mask):
    n_chunks = k.shape[0] // SB_BLOCK
    z = lax.dot_general(q, k, (((1,), (1,)), ((), ())), preferred_element_type=F32) * SB_SCALE + bias
    t = jnp.log(1.0 + jnp.exp(-jnp.abs(z)))
    log_rest = -(jnp.maximum(z, 0.0) + t)
    log_beta = jnp.minimum(z, 0.0) - t
    if mask is not None:
        log_rest = jnp.where(mask, log_rest, 0.0)
    hi = log_rest.astype(BF16)
    lo = (log_rest - hi.astype(F32)).astype(BF16)
    suffix = [None] * n_chunks
    for c in reversed(range(n_chunks)):
        cols = slice(c * SB_BLOCK, (c + 1) * SB_BLOCK)
        sums = jnp.dot(jnp.concatenate([hi[:, cols], lo[:, cols]], axis=1), suffix_ext,
                       preferred_element_type=F32)
        suffix[c] = sums[:, :SB_BLOCK] + carry
        carry = carry + sums[:, SB_BLOCK:]
    a = jnp.exp(log_beta + jnp.concatenate(suffix, axis=1))
    if mask is not None:
        a = jnp.where(mask, a, 0.0)
    acc = acc + jnp.dot(a.astype(BF16), v, preferred_element_type=F32)
    return carry, acc


def _suffix_matrix_ext():
    r = lax.broadcasted_iota(I32, (2 * SB_BLOCK, 2 * SB_BLOCK), 0) % SB_BLOCK
    c = lax.broadcasted_iota(I32, (2 * SB_BLOCK, 2 * SB_BLOCK), 1)
    return jnp.logical_or(r > c, c >= SB_BLOCK).astype(BF16)


def _attn_prompt_kernel(bias_ref, q_ref, k_ref, v_ref, o_ref, qb_ref, kb_ref, vb_ref, carry_ref, acc_ref,
                        *, n_tiles, tq):
    bias = bias_ref[pl.program_id(1)]
    qb_ref[...] = q_ref[...].astype(BF16)
    kb_ref[...] = k_ref[...].astype(BF16)
    vb_ref[...] = v_ref[...].astype(BF16)
    suffix_ext = _suffix_matrix_ext()
    r = lax.broadcasted_iota(I32, (tq, tq), 0)
    c = lax.broadcasted_iota(I32, (tq, tq), 1)
    diag_mask = c < r

    def q_tile(qi, _):
        q0 = pl.multiple_of(qi * tq, tq)
        q = qb_ref[pl.ds(q0, tq), :]
        carry, acc = _sb_tile(q, kb_ref[pl.ds(q0, tq), :], vb_ref[pl.ds(q0, tq), :], bias,
                              jnp.zeros((tq, SB_BLOCK), F32), jnp.zeros((tq, HEAD_DIM), F32),
                              suffix_ext, diag_mask)
        carry_ref[...] = carry
        acc_ref[...] = acc

        def k_tile(n, _):
            k0 = pl.multiple_of((qi - 1 - n) * tq, tq)
            carry, acc = _sb_tile(q, kb_ref[pl.ds(k0, tq), :], vb_ref[pl.ds(k0, tq), :], bias,
                                  carry_ref[...], acc_ref[...], suffix_ext, None)
            carry_ref[...] = carry
            acc_ref[...] = acc
            return 0

        lax.fori_loop(0, qi, k_tile, 0)
        o_ref[pl.ds(q0, tq), :] = acc_ref[...]
        return 0

    lax.fori_loop(0, n_tiles, q_tile, 0)


def _attn_prompt(proj, sb_bias, n_batch, seq, n_heads, col_q, col_k, col_v):
    tq = min(SB_PROMPT_TILE, seq)
    assert seq % tq == 0
    blk_spec = lambda col: pl.BlockSpec((seq, HEAD_DIM), lambda b, h: (b, col // HEAD_DIM + h))
    return pl.pallas_call(
        functools.partial(_attn_prompt_kernel, n_tiles=seq // tq, tq=tq),
        grid=(n_batch, n_heads),
        in_specs=[pl.BlockSpec(memory_space=pltpu.SMEM), blk_spec(col_q), blk_spec(col_k), blk_spec(col_v)],
        out_specs=pl.BlockSpec((seq, HEAD_DIM), lambda b, h: (b, h)),
        out_shape=jax.ShapeDtypeStruct((n_batch * seq, n_heads * HEAD_DIM), F32),
        scratch_shapes=[pltpu.VMEM((seq, HEAD_DIM), BF16)] * 3
                       + [pltpu.VMEM((tq, SB_BLOCK), F32), pltpu.VMEM((tq, HEAD_DIM), F32)],
        compiler_params=_cparams(("arbitrary", "arbitrary"), 40),
        name="attn_prompt",
    )(sb_bias, proj, proj, proj)


def _attn_sample_kernel(pt_ref, q_ref, kn_ref, vn_ref, bias_ref, *refs, n_heads, n_q, pages_per_step):
    k_refs = refs[:pages_per_step]
    v_refs = refs[pages_per_step:2 * pages_per_step]
    o_ref = refs[2 * pages_per_step]
    qbd_ref, acc_ref, carry_ref = refs[2 * pages_per_step + 1:]
    j = pl.program_id(1)
    rows = n_heads * n_q
    width = n_heads * HEAD_DIM
    page = SB_BLOCK
    suffix_ext = _suffix_matrix_ext()
    bias = bias_ref[...]
    row_head = lax.broadcasted_iota(I32, (rows, width), 0) // n_q
    col_head = lax.broadcasted_iota(I32, (rows, width), 1) // HEAD_DIM

    @pl.when(j == 0)
    def _():
        qbd_ref[...] = jnp.where(row_head == col_head, q_ref[...], 0.0).astype(BF16)
        r = lax.broadcasted_iota(I32, (rows, page), 0) % n_q
        c = lax.broadcasted_iota(I32, (rows, page), 1)
        carry, acc = _sb_tile(qbd_ref[...], kn_ref[...].astype(BF16), vn_ref[...].astype(BF16), bias,
                              jnp.zeros((rows, page), F32), jnp.zeros((rows, width), F32), suffix_ext,
                              c < r)
        carry_ref[...] = carry
        acc_ref[...] = acc

    def head_major(page_ref):
        return jnp.concatenate([page_ref[pl.ds(h, page, stride=n_heads), :].astype(BF16)
                                for h in range(n_heads)], axis=1)

    k = jnp.concatenate([head_major(k_refs[p]) for p in reversed(range(pages_per_step))], axis=0)
    v = jnp.concatenate([head_major(v_refs[p]) for p in reversed(range(pages_per_step))], axis=0)
    carry, acc = _sb_tile(qbd_ref[...], k, v, bias, carry_ref[...], acc_ref[...], suffix_ext, None)
    carry_ref[...] = carry
    acc_ref[...] = acc

    @pl.when(j == pl.num_programs(1) - 1)
    def _():
        own = jnp.where(row_head == col_head, acc, 0.0)
        out = own[0:n_q, :]
        for h in range(1, n_heads):
            out = out + own[h * n_q:(h + 1) * n_q, :]
        o_ref[...] = out


def _attn_sample(q, k_new, v_new, bias_rows, cache_k, cache_v, page_table, layer, n_heads):
    n_b, rows, width = q.shape
    n_q = rows // n_heads
    n_pages = page_table.shape[1]
    page = SB_BLOCK
    pps = SAMPLE_PAGES_PER_STEP
    assert n_pages % pps == 0
    n_steps = n_pages // pps

    def page_spec(p):
        return pl.BlockSpec((None, None, page * n_heads, HEAD_DIM),
                            lambda b, j, pt: (pt[b, n_pages - 1 - (j * pps + p)], layer, 0, 0))

    grid_spec = pltpu.PrefetchScalarGridSpec(
        num_scalar_prefetch=1,
        grid=(n_b, n_steps),
        in_specs=[pl.BlockSpec((None, rows, width), lambda b, j, pt: (b, 0, 0)),
                  pl.BlockSpec((None, page, width), lambda b, j, pt: (b, 0, 0)),
                  pl.BlockSpec((None, page, width), lambda b, j, pt: (b, 0, 0)),
                  pl.BlockSpec((rows, 1), lambda b, j, pt: (0, 0))]
                 + [page_spec(p) for p in range(pps)] * 1
                 + [page_spec(p) for p in range(pps)],
        out_specs=pl.BlockSpec((None, n_q, width), lambda b, j, pt: (b, 0, 0)),
        scratch_shapes=[pltpu.VMEM((rows, width), BF16), pltpu.VMEM((rows, width), F32),
                        pltpu.VMEM((rows, page), F32)],
    )
    return pl.pallas_call(
        functools.partial(_attn_sample_kernel, n_heads=n_heads, n_q=n_q, pages_per_step=pps),
        grid_spec=grid_spec,
        out_shape=jax.ShapeDtypeStruct((n_b, n_q, width), F32),
        compiler_params=_cparams(("arbitrary", "arbitrary"), 48),
        name="attn_sample",
    )(page_table, q, k_new, v_new, bias_rows, *([cache_k] * pps), *([cache_v] * pps))


def _spatial_gate(u_act, v_norm, w_of_head, b_of_head, keep, n_heads):
    outs = []
    for h in range(n_heads):
        w = jnp.where(keep, w_of_head(h), 0.0).astype(BF16)
        v_h = v_norm[:, h * HEAD_DIM:(h + 1) * HEAD_DIM].astype(BF16)
        outs.append(jnp.dot(w, v_h, preferred_element_type=F32) + b_of_head(h))
    return u_act * jnp.concatenate(outs, axis=1)


def _group_a_prompt_kernel(u_ref, v_ref, ws_ref, bs_ref, lg_ref, lb_ref, og_ref, o_ref, *, n_heads):
    u_act = _gelu(u_ref[...])
    v_norm = _layer_norm(_gelu(v_ref[...]), lg_ref[...], lb_ref[...])
    r = lax.broadcasted_iota(I32, (CHUNK, CHUNK), 0)
    c = lax.broadcasted_iota(I32, (CHUNK, CHUNK), 1)
    y = _spatial_gate(u_act, v_norm, lambda h: ws_ref[h], lambda h: bs_ref[h], c <= r, n_heads)
    o_ref[...] = _rms_norm(y, og_ref[...]).astype(BF16)


def _group_a_prompt(proj, w_s, b_s, lnv_g, lnv_b, og, n_rows, d_a):
    n_heads = d_a // HEAD_DIM
    vec = pl.BlockSpec((1, d_a), lambda c: (0, 0))
    return pl.pallas_call(
        functools.partial(_group_a_prompt_kernel, n_heads=n_heads),
        grid=(n_rows // CHUNK,),
        in_specs=[pl.BlockSpec((CHUNK, d_a), lambda c: (c, 0)),
                  pl.BlockSpec((CHUNK, d_a), lambda c: (c, 1)),
                  pl.BlockSpec((n_heads, CHUNK, CHUNK), lambda c: (0, 0, 0)),
                  pl.BlockSpec((n_heads, CHUNK, 1), lambda c: (0, 0, 0)),
                  vec, vec, vec],
        out_specs=pl.BlockSpec((CHUNK, d_a), lambda c: (c, 0)),
        out_shape=jax.ShapeDtypeStruct((n_rows, d_a), BF16),
        compiler_params=_cparams(("arbitrary",), 16),
        name="group_a_prompt",
    )(proj, proj, w_s, b_s.reshape(n_heads, CHUNK, 1), lnv_g.reshape(1, d_a), lnv_b.reshape(1, d_a),
      og.reshape(1, d_a))


CONV_HALO = 8


def _group_c_prompt_kernel(gb_ref, gc_ref, xc_ref, gch_ref, xch_ref, cw_ref, og_ref, o_ref, new_ref,
                           *, tc):
    t = pl.program_id(1)
    p = gc_ref[...] * xc_ref[...]
    halo = jnp.where(t > 0, gch_ref[...] * xch_ref[...], 0.0)
    ext = jnp.concatenate([halo, p], axis=0)
    p1 = pltpu.roll(ext, 1, 0)[CONV_HALO:, :]
    p2 = pltpu.roll(ext, 2, 0)[CONV_HALO:, :]
    cw = cw_ref[...]
    conv = cw[2:3, :] * p
    conv = conv + cw[0:1, :] * p2
    conv = conv + cw[1:2, :] * p1
    o_ref[...] = _rms_norm(gb_ref[...] * conv, og_ref[...]).astype(BF16)

    @pl.when(t == pl.num_programs(1) - 1)
    def _():
        new_ref[...] = p[tc - 2:tc, :]


def _group_c_prompt(proj, conv_w, og, n_batch, seq, d_c, col_gb):
    cb = col_gb // d_c
    tc = min(512, seq)
    n_t = seq // tc
    tile = lambda col: pl.BlockSpec((tc, d_c), lambda b, t: (b * n_t + t, col))
    halo = lambda col: pl.BlockSpec(
        (CONV_HALO, d_c), lambda b, t: (jnp.maximum((b * seq + t * tc) // CONV_HALO - 1, 0), col))
    return pl.pallas_call(
        functools.partial(_group_c_prompt_kernel, tc=tc),
        grid=(n_batch, n_t),
        in_specs=[tile(cb), tile(cb + 1), tile(cb + 2), halo(cb + 1), halo(cb + 2),
                  pl.BlockSpec((3, d_c), lambda b, t: (0, 0)), pl.BlockSpec((1, d_c), lambda b, t: (0, 0))],
        out_specs=[pl.BlockSpec((tc, d_c), lambda b, t: (b * n_t + t, 0)),
                   pl.BlockSpec((None, 2, d_c), lambda b, t: (b, 0, 0))],
        out_shape=[jax.ShapeDtypeStruct((n_batch * seq, d_c), BF16),
                   jax.ShapeDtypeStruct((n_batch, 2, d_c), F32)],
        compiler_params=_cparams(("arbitrary", "arbitrary"), 32),
        name="group_c_prompt",
    )(proj, proj, proj, proj, proj, conv_w, og.reshape(1, d_c))


def _mixer_cat_prompt_kernel(ya_ref, yb_ref, yc_ref, og_ref, o_ref, *, d_a, d_b):
    o_ref[:, 0:d_a] = ya_ref[...]
    o_ref[:, d_a:d_a + d_b] = _rms_norm(yb_ref[...], og_ref[...]).astype(BF16)
    o_ref[:, d_a + d_b:] = yc_ref[...]


def _mixer_cat_prompt(ya, yb, yc, og_b, tm):
    n_rows, d_a = ya.shape
    d_b = yb.shape[1]
    d_c = yc.shape[1]
    return pl.pallas_call(
        functools.partial(_mixer_cat_prompt_kernel, d_a=d_a, d_b=d_b),
        grid=(n_rows // tm,),
        in_specs=[pl.BlockSpec((tm, d_a), lambda i: (i, 0)), pl.BlockSpec((tm, d_b), lambda i: (i, 0)),
                  pl.BlockSpec((tm, d_c), lambda i: (i, 0)), pl.BlockSpec((1, d_b), lambda i: (0, 0))],
        out_specs=pl.BlockSpec((tm, d_a + d_b + d_c), lambda i: (i, 0)),
        out_shape=jax.ShapeDtypeStruct((n_rows, d_a + d_b + d_c), BF16),
        compiler_params=_cparams(("arbitrary",), 32),
        name="mixer_cat_prompt",
    )(ya, yb, yc, og_b.reshape(1, d_b))


def _mixer_sample_kernel(proj_ref, yb_ref, wt_ref, bt_ref, lg_ref, lb_ref, cw_ref, e1_ref, e2_ref,
                         og_ref, o_ref, va_ref, p_ref, *, n_q, d_a, d_b, d_c):
    n_rows = proj_ref.shape[0]
    n_heads_a = d_a // HEAD_DIM
    c_gb = 2 * d_a + 3 * d_b
    og = og_ref[...]
    u_act = _gelu(proj_ref[:, 0:d_a])
    v_norm = _layer_norm(_gelu(proj_ref[:, d_a:2 * d_a]), lg_ref[...], lb_ref[...])
    va_ref[...] = v_norm
    r = lax.broadcasted_iota(I32, (n_rows, n_rows), 0)
    c = lax.broadcasted_iota(I32, (n_rows, n_rows), 1)
    keep = jnp.logical_and(c <= r, r // n_q == c // n_q)
    ya = _spatial_gate(u_act, v_norm, lambda h: wt_ref[h], lambda h: bt_ref[h], keep, n_heads_a)
    o_ref[:, 0:d_a] = _rms_norm(ya, og[:, 0:d_a]).astype(BF16)
    o_ref[:, d_a:d_a + d_b] = _rms_norm(yb_ref[...], og[:, d_a:d_a + d_b]).astype(BF16)
    p = proj_ref[:, c_gb + d_c:c_gb + 2 * d_c] * proj_ref[:, c_gb + 2 * d_c:c_gb + 3 * d_c]
    p_ref[...] = p
    pos = lax.broadcasted_iota(I32, p.shape, 0) % n_q
    p1 = jnp.where(pos >= 1, pltpu.roll(p, 1, 0), e1_ref[...])
    p2 = jnp.where(pos >= 2, pltpu.roll(p, 2, 0), e2_ref[...])
    cw = cw_ref[...]
    conv = cw[2:3, :] * p
    conv = conv + cw[0:1, :] * p2
    conv = conv + cw[1:2, :] * p1
    yc = proj_ref[:, c_gb:c_gb + d_c] * conv
    o_ref[:, d_a + d_b:] = _rms_norm(yc, og[:, d_a + d_b:]).astype(BF16)


def _mixer_sample(proj, row0, yb, w_s, b_s, lnv_g, lnv_b, conv_w, state, og, n_b, n_q, d_a, d_b, d_c):
    n = n_b * n_q
    assert row0 % n == 0
    d_in = proj.shape[1]
    n_heads_a = d_a // HEAD_DIM
    d_mix = d_a + d_b + d_c
    wt = jnp.tile(w_s[:, :n_q, :n_q], (1, n_b, n_b))
    bt = jnp.tile(b_s[:, :n_q], (1, n_b)).reshape(n_heads_a, n, 1)
    e1 = jnp.repeat(state[:, 1:2, :], n_q, axis=1).reshape(n, d_c)
    e2 = jnp.tile(state, (1, n_q // 2, 1)).reshape(n, d_c)
    full = lambda shape: pl.BlockSpec(shape, lambda i: (0,) * len(shape))
    return pl.pallas_call(
        functools.partial(_mixer_sample_kernel, n_q=n_q, d_a=d_a, d_b=d_b, d_c=d_c),
        grid=(1,),
        in_specs=[pl.BlockSpec((n, d_in), lambda i: (row0 // n, 0)), full((n, d_b)),
                  full((n_heads_a, n, n)), full((n_heads_a, n, 1)), full((1, d_a)), full((1, d_a)),
                  full((3, d_c)), full((n, d_c)), full((n, d_c)), full((1, d_mix))],
        out_specs=[full((n, d_mix)), full((n, d_a)), full((n, d_c))],
        out_shape=[jax.ShapeDtypeStruct((n, d_mix), BF16), jax.ShapeDtypeStruct((n, d_a), F32),
                   jax.ShapeDtypeStruct((n, d_c), F32)],
        compiler_params=_cparams(("arbitrary",), 16),
        name="mixer_sample",
    )(proj, yb, wt, bt, lnv_g.reshape(1, d_a), lnv_b.reshape(1, d_a), conv_w, e1, e2,
      og.reshape(1, d_mix))


def kernel(x_prompt, x_sample, cache_k, cache_v, state_conv, page_table, p_prompt, p_sample, w_in, w_spatial, b_spatial, lnv_g, lnv_b, conv_w, sb_bias, out_norm_g, w_out, ln1_g, ln1_b, ln2_g, ln2_b, w_ffn_gate, w_ffn_up, w_ffn_down, w_router, w_exp_gate, w_exp_up, w_exp_down, w_ple_gate, w_ple_proj):
    n_b, seq, d = x_prompt.shape
    n_db, n_q, _ = x_sample.shape
    depth = w_in.shape[0]
    n_heads_b = cache_k.shape[3]
    page = cache_k.shape[2]
    assert page == SB_BLOCK and cache_k.shape[4] == HEAD_DIM and seq % SB_BLOCK == 0
    d_a = lnv_g.shape[1]
    d_b = n_heads_b * HEAD_DIM
    d_c = conv_w.shape[2]
    n_exp = w_router.shape[-1]
    n_p = n_b * seq
    n_s = n_db * n_q
    alpha = float((2 * depth) ** 0.25)
    col_q, col_k, col_v = 2 * d_a, 2 * d_a + d_b, 2 * d_a + 2 * d_b
    col_gb = 2 * d_a + 3 * d_b

    xf = jnp.concatenate([x_prompt.reshape(n_p, d), x_sample.reshape(n_s, d)], axis=0)
    xb = xf.astype(BF16)
    ck = cache_k.reshape(cache_k.shape[0], depth, page * n_heads_b, HEAD_DIM)
    cv = cache_v.reshape(cache_v.shape[0], depth, page * n_heads_b, HEAD_DIM)
    page_table = page_table.astype(I32)

    kp, vp, convp, ks, vs, convs, vas = [], [], [], [], [], [], []
    for i in range(depth):
        og = out_norm_g[i]
        proj = _mm(xb, w_in, i, F32, 512, 1408, "proj_in")
        ya = _group_a_prompt(proj, w_spatial[i], b_spatial[i], lnv_g[i], lnv_b[i], og[:d_a], n_p, d_a)
        yb = _attn_prompt(proj, sb_bias[i], n_b, seq, n_heads_b, col_q, col_k, col_v)
        yc, conv_new_p = _group_c_prompt(proj, conv_w[i], og[d_a + d_b:], n_b, seq, d_c, col_gb)
        ycat_p = _mixer_cat_prompt(ya, yb, yc, og[d_a:d_a + d_b], 512)
        proj_s = proj[n_p:]
        q_s = proj_s[:, col_q:col_k].reshape(n_db, n_q, d_b)
        k_s = proj_s[:, col_k:col_v].reshape(n_db, n_q, d_b)
        v_s = proj_s[:, col_v:col_gb].reshape(n_db, n_q, d_b)
        pad = ((0, 0), (0, page - n_q), (0, 0))
        bias_rows = jnp.repeat(sb_bias[i], n_q).reshape(n_heads_b * n_q, 1)
        yb_s = _attn_sample(jnp.tile(q_s, (1, n_heads_b, 1)), jnp.pad(k_s, pad), jnp.pad(v_s, pad),
                            bias_rows, ck, cv, page_table, i, n_heads_b)
        ycat_s, va_s, pc_s = _mixer_sample(proj, n_p, yb_s.reshape(n_s, d_b), w_spatial[i], b_spatial[i],
                                           lnv_g[i], lnv_b[i], conv_w[i], state_conv[:, i], og,
                                           n_db, n_q, d_a, d_b, d_c)
        ycat = jnp.concatenate([ycat_p, ycat_s], axis=0)
        kp.append(proj[:n_p, col_k:col_v].reshape(n_b, seq, n_heads_b, HEAD_DIM))
        vp.append(proj[:n_p, col_v:col_gb].reshape(n_b, seq, n_heads_b, HEAD_DIM))
        convp.append(conv_new_p)
        ks.append(k_s.reshape(n_db, n_q, n_heads_b, HEAD_DIM))
        vs.append(v_s.reshape(n_db, n_q, n_heads_b, HEAD_DIM))
        convs.append(pc_s.reshape(n_db, n_q, d_c)[:, n_q - 2:])
        vas.append(va_s.reshape(n_db, n_q, d_a))
        h = _mm(ycat, w_out, i, F32, 1024, 1024, "proj_out")
        jl = i // 2
        if i % 2 == 0:
            x1f, x1b = _ln_res(xf, h, ln1_g[i], ln1_b[i], alpha, 256)
            hid = _swiglu_up(x1b, w_ffn_gate, w_ffn_up, jl, 1024, 512)
            f = _mm(hid, w_ffn_down, jl, F32, 512, 512, "ffn_down")
            x2f, x2b = _ln_res(x1f, f, ln2_g[i], ln2_b[i], alpha, 256)
        else:
            x1f, x1b, idx, gates = _ln_res(xf, h, ln1_g[i], ln1_b[i], alpha, 256, w_router=w_router[jl])
            dest, row_token, tile_expert, n_used = _route(idx, n_exp, MOE_TILE)
            xs = _moe_dispatch(x1f, row_token, n_used)
            hid = _moe_up(xs, w_exp_gate, w_exp_up, jl, tile_expert, n_used, 1024)
            ys = _moe_down(hid, w_exp_down, jl, tile_expert, n_used, 512)
            x2f, x2b = _ln_res_combine(x1f, ys, dest, gates, ln2_g[i], ln2_b[i], alpha, 128)
        p_i = jnp.concatenate([p_prompt[i].reshape(n_p, -1), p_sample[i].reshape(n_s, -1)], axis=0)
        xf, xb = _ple(x2b, x2f, p_i, w_ple_gate, w_ple_proj, i, 512, 1024)

    return (xf[:n_p].reshape(n_b, seq, d), xf[n_p:].reshape(n_db, n_q, d),
            jnp.stack(kp, axis=1), jnp.stack(vp, axis=1), jnp.stack(convp, axis=1),
            jnp.stack(ks, axis=1), jnp.stack(vs, axis=1), jnp.stack(convs, axis=1),
            jnp.stack(vas, axis=1))
```
